```python
import math
import jax, jax.numpy as jnp
from jax import lax
import numpy as np

D_MODEL = 2048
BATCH = 4
SEQ = 2048
DEPTH = 1
DEC_BATCH = 32
DEC_SEQ = 8
PAST_LEN = 16384
PAGE_SIZE = 128

H_A = D_MODEL // 256
D_A = 64
H_B = D_MODEL // 128
D_B = 64
W_A = H_A * 2 * D_A
W_B = H_B * D_B
KV_WIDTH = W_A + W_B
IN_WIDTH = 3 * KV_WIDTH + 2 * D_MODEL
MOBA_BLOCK = 256
MOBA_TOPK = 3
N_BUCKETS = 32
MAX_DISTANCE = 128
D_FF = ((8 * D_MODEL // 3 + 127) // 128) * 128
Q_BLOCK_DIFF = 128
Q_BLOCK_MOBA = 32
EPS = 1e-6

kernel_name = 'diff_moba_macaron_gated_decoder_step'


def rmsnorm(x, g):
    xf = x.astype(jnp.float32)
    y = xf * lax.rsqrt(jnp.mean(xf * xf, axis=-1, keepdims=True) + EPS)
    return (y * g.astype(jnp.float32)).astype(x.dtype)


def swiglu_half(x, g, w_gate, w_up, w_down):
    h = rmsnorm(x, g)
    return x + 0.5 * ((jax.nn.silu(h @ w_gate) * (h @ w_up)) @ w_down)


def rel_bucket(dist):
    n = jnp.maximum(dist, 0)
    max_exact = N_BUCKETS // 2
    nf = jnp.maximum(n, max_exact).astype(jnp.float32)
    log_b = max_exact + (jnp.log(nf / max_exact) / math.log(MAX_DISTANCE / max_exact)
                         * (N_BUCKETS - max_exact)).astype(jnp.int32)
    return jnp.where(n < max_exact, n, jnp.minimum(log_b, N_BUCKETS - 1))


def split_heads(t):
    lead = t.shape[:-1]
    return (t[..., :W_A].reshape(lead + (H_A, 2 * D_A)),
            t[..., W_A:].reshape(lead + (H_B, D_B)))


def project(h, w_in):
    p = h @ w_in
    q = p[..., :KV_WIDTH]
    k = p[..., KV_WIDTH:2 * KV_WIDTH]
    v = p[..., 2 * KV_WIDTH:3 * KV_WIDTH]
    g_a = p[..., 3 * KV_WIDTH:3 * KV_WIDTH + D_MODEL]
    g_b = p[..., 3 * KV_WIDTH + D_MODEL:]
    return q, k, v, g_a, g_b


def merge(o_a, o_b, g_a, g_b, w_up_a, w_up_b, w_out):
    m = jax.nn.sigmoid(g_a) * (o_a @ w_up_a) + jax.nn.sigmoid(g_b) * (o_b @ w_up_b)
    return m @ w_out


def diff_attend(q, k, v, q_pos, lam, lam_init, sub_gain, bias_tab):
    tq, n_keys = q.shape[0], k.shape[0]
    qb = min(Q_BLOCK_DIFF, tq)
    nq = tq // qb
    k_pos = jnp.arange(n_keys, dtype=jnp.int32)
    k1, k2 = k[..., :D_A], k[..., D_A:]
    scale = D_A ** -0.5

    def block(args):
        qc, pc = args
        s1 = jnp.einsum('qhd,khd->hqk', qc[..., :D_A], k1).astype(jnp.float32) * scale
        s2 = jnp.einsum('qhd,khd->hqk', qc[..., D_A:], k2).astype(jnp.float32) * scale
        dist = pc[:, None] - k_pos[None, :]
        bias = jnp.moveaxis(bias_tab[rel_bucket(dist)].astype(jnp.float32), -1, 0)
        causal = (dist >= 0)[None]
        p1 = jax.nn.softmax(jnp.where(causal, s1 + bias[:H_A], -jnp.inf), axis=-1)
        p2 = jax.nn.softmax(jnp.where(causal, s2 + bias[H_A:], -jnp.inf), axis=-1)
        attn = (p1 - lam * p2).astype(v.dtype)
        o = jnp.einsum('hqk,khd->qhd', attn, v)
        return rmsnorm(o, sub_gain) * (1.0 - lam_init)

    out = lax.map(block, (q.reshape(nq, qb, H_A, 2 * D_A), q_pos.reshape(nq, qb)))
    return out.reshape(tq, W_A)


def moba_attend(q, k, v, q_pos, bias_tab):
    tq, n_keys = q.shape[0], k.shape[0]
    nbk = -(-n_keys // MOBA_BLOCK)
    pad = nbk * MOBA_BLOCK - n_keys
    kb = jnp.pad(k, ((0, pad), (0, 0), (0, 0))).reshape(nbk, MOBA_BLOCK, H_B, D_B).transpose(2, 0, 1, 3)
    vb = jnp.pad(v, ((0, pad), (0, 0), (0, 0))).reshape(nbk, MOBA_BLOCK, H_B, D_B).transpose(2, 0, 1, 3)
    k_mean = jnp.mean(kb.astype(jnp.float32), axis=2)
    n_sel = min(MOBA_TOPK, nbk)
    qb = min(Q_BLOCK_MOBA, tq)
    nq = tq // qb
    blk_ids = jnp.arange(nbk, dtype=jnp.int32)
    offs = jnp.arange(MOBA_BLOCK, dtype=jnp.int32)
    heads = jnp.arange(H_B, dtype=jnp.int32)
    scale = D_B ** -0.5

    def block(args):
        qc, pc = args
        own = pc[0] // MOBA_BLOCK
        gate = jnp.einsum('qhd,hnd->qhn', qc.astype(jnp.float32), k_mean)
        fully_past = blk_ids[None, None, :] < (pc // MOBA_BLOCK)[:, None, None]
        gate = jnp.where(fully_past, gate, -jnp.inf)
        top_s, idx = lax.top_k(gate, n_sel)
        valid = jnp.isfinite(top_s)
        k_sel = kb[heads[None, :, None], idx]
        v_sel = vb[heads[None, :, None], idx]
        s_sel = jnp.einsum('qhd,qhnkd->qhnk', qc, k_sel).astype(jnp.float32) * scale
        kpos_sel = idx[..., None] * MOBA_BLOCK + offs
        b_sel = bias_tab[rel_bucket(pc[:, None, None, None] - kpos_sel), heads[None, :, None, None]]
        s_sel = jnp.where(valid[..., None], s_sel + b_sel.astype(jnp.float32), -jnp.inf)
        k_own = lax.dynamic_index_in_dim(kb, own, axis=1, keepdims=False)
        v_own = lax.dynamic_index_in_dim(vb, own, axis=1, keepdims=False)
        s_own = jnp.einsum('qhd,hkd->qhk', qc, k_own).astype(jnp.float32) * scale
        dist_own = pc[:, None] - (own * MOBA_BLOCK + offs)[None, :]
        b_own = jnp.moveaxis(bias_tab[rel_bucket(dist_own)].astype(jnp.float32), -1, 1)
        s_own = jnp.where((dist_own >= 0)[:, None, :], s_own + b_own, -jnp.inf)
        logits = jnp.concatenate([s_sel.reshape(qb, H_B, n_sel * MOBA_BLOCK), s_own], axis=-1)
        p = jax.nn.softmax(logits, axis=-1).astype(v.dtype)
        p_sel = p[..., :n_sel * MOBA_BLOCK].reshape(qb, H_B, n_sel, MOBA_BLOCK)
        p_own = p[..., n_sel * MOBA_BLOCK:]
        return (jnp.einsum('qhnk,qhnkd->qhd', p_sel, v_sel)
                + jnp.einsum('qhk,hkd->qhd', p_own, v_own))

    out = lax.map(block, (q.reshape(nq, qb, H_B, D_B), q_pos.reshape(nq, qb)))
    return out.reshape(tq, W_B)


def mixers(q, k, v, q_pos, lam, lam_init, sub_gain, rel_bias):
    q_a, q_b = split_heads(q)
    k_a, k_b = split_heads(k)
    v_a, v_b = split_heads(v)
    o_a = diff_attend(q_a, k_a, v_a, q_pos, lam, lam_init, sub_gain, rel_bias[:, :2 * H_A])
    o_b = moba_attend(q_b, k_b, v_b, q_pos, rel_bias[:, 2 * H_A:])
    return o_a, o_b


def setup_inputs(seed: int = 0) -> dict:
    key = jax.random.key(seed)
    ks = jax.random.split(key, 24)
    n_pages = PAST_LEN // PAGE_SIZE
    n_used = DEC_BATCH * n_pages
    n_pool = n_used + max(1, n_used // 4)

    def nrm(k, shape, scale):
        return jax.random.normal(k, shape, jnp.float32) * scale

    def gain(k, shape):
        return 1.0 + 0.01 * jax.random.normal(k, shape, jnp.float32)

    page_table = jax.random.permutation(ks[4], n_pool)[:n_used].reshape(DEC_BATCH, n_pages).astype(jnp.int32)
    return {
        'x_prompt': nrm(ks[0], (BATCH, SEQ, D_MODEL), 1.0),
        'x_sample': nrm(ks[1], (DEC_BATCH, DEC_SEQ, D_MODEL), 1.0),
        'cache_k': nrm(ks[2], (n_pool, DEPTH, PAGE_SIZE, KV_WIDTH), 1.0),
        'cache_v': nrm(ks[3], (n_pool, DEPTH, PAGE_SIZE, KV_WIDTH), 1.0),
        'page_table': page_table,
        'rel_bias': nrm(ks[5], (N_BUCKETS, 2 * H_A + H_B), 0.5),
        'ffn1_norm': gain(ks[6], (DEPTH, D_MODEL)),
        'ffn1_w_gate': nrm(ks[7], (DEPTH, D_MODEL, D_FF), D_MODEL ** -0.5),
        'ffn1_w_up': nrm(ks[8], (DEPTH, D_MODEL, D_FF), D_MODEL ** -0.5),
        'ffn1_w_down': nrm(ks[9], (DEPTH, D_FF, D_MODEL), D_FF ** -0.5),
        'mix_norm': gain(ks[10], (DEPTH, D_MODEL)),
        'w_in': nrm(ks[11], (DEPTH, D_MODEL, IN_WIDTH), D_MODEL ** -0.5),
        'lambda_q1': nrm(ks[12], (DEPTH, D_A), 0.1),
        'lambda_k1': nrm(ks[13], (DEPTH, D_A), 0.1),
        'lambda_q2': nrm(ks[14], (DEPTH, D_A), 0.1),
        'lambda_k2': nrm(ks[15], (DEPTH, D_A), 0.1),
        'subln_gain': gain(ks[16], (DEPTH, 2 * D_A)),
        'w_up_a': nrm(ks[17], (DEPTH, W_A, D_MODEL), W_A ** -0.5),
        'w_up_b': nrm(ks[18], (DEPTH, W_B, D_MODEL), W_B ** -0.5),
        'w_out': nrm(ks[19], (DEPTH, D_MODEL, D_MODEL), D_MODEL ** -0.5),
        'ffn2_norm': gain(ks[20], (DEPTH, D_MODEL)),
        'ffn2_w_gate': nrm(ks[21], (DEPTH, D_MODEL, D_FF), D_MODEL ** -0.5),
        'ffn2_w_up': nrm(ks[22], (DEPTH, D_MODEL, D_FF), D_MODEL ** -0.5),
        'ffn2_w_down': nrm(ks[23], (DEPTH, D_FF, D_MODEL), D_FF ** -0.5),
        'final_norm': gain(jax.random.fold_in(key, 99), (D_MODEL,)),
    }


def reference(x_prompt, x_sample, cache_k, cache_v, page_table, rel_bias,
              ffn1_norm, ffn1_w_gate, ffn1_w_up, ffn1_w_down, mix_norm, w_in,
              lambda_q1, lambda_k1, lambda_q2, lambda_k2, subln_gain, w_up_a, w_up_b, w_out,
              ffn2_norm, ffn2_w_gate, ffn2_w_up, ffn2_w_down, final_norm):
    seq = x_prompt.shape[1]
    dec_seq = x_sample.shape[1]
    past_len = page_table.shape[1] * cache_k.shape[2]
    pos_p = jnp.arange(seq, dtype=jnp.int32)
    pos_s = past_len + jnp.arange(dec_seq, dtype=jnp.int32)
    hp, hs = x_prompt, x_sample
    k_p_rows, v_p_rows, k_s_rows, v_s_rows = [], [], [], []
    for l in range(DEPTH):
        lam_init = 0.8 - 0.6 * math.exp(-0.3 * l)
        lam = (jnp.exp(jnp.sum(lambda_q1[l].astype(jnp.float32) * lambda_k1[l].astype(jnp.float32)))
               - jnp.exp(jnp.sum(lambda_q2[l].astype(jnp.float32) * lambda_k2[l].astype(jnp.float32)))
               + lam_init)
        sub_g = subln_gain[l]

        hp = swiglu_half(hp, ffn1_norm[l], ffn1_w_gate[l], ffn1_w_up[l], ffn1_w_down[l])
        hs = swiglu_half(hs, ffn1_norm[l], ffn1_w_gate[l], ffn1_w_up[l], ffn1_w_down[l])

        qp, kp, vp, gap, gbp = project(rmsnorm(hp, mix_norm[l]), w_in[l])
        qs, ks_new, vs_new, gas, gbs = project(rmsnorm(hs, mix_norm[l]), w_in[l])

        def mix_seq(q, k, v, q_pos):
            return mixers(q, k, v, q_pos, lam, lam_init, sub_g, rel_bias)

        o_ap, o_bp = jax.vmap(mix_seq, in_axes=(0, 0, 0, None))(qp, kp, vp, pos_p)

        def sample_one(args):
            q, k_new, v_new, pt = args
            k_all = jnp.concatenate([cache_k[pt, l].reshape(-1, KV_WIDTH), k_new], axis=0)
            v_all = jnp.concatenate([cache_v[pt, l].reshape(-1, KV_WIDTH), v_new], axis=0)
            return mix_seq(q, k_all, v_all, pos_s)

        o_as, o_bs = lax.map(sample_one, (qs, ks_new, vs_new, page_table))

        hp = hp + merge(o_ap, o_bp, gap, gbp, w_up_a[l], w_up_b[l], w_out[l])
        hs = hs + merge(o_as, o_bs, gas, gbs, w_up_a[l], w_up_b[l], w_out[l])

        hp = swiglu_half(hp, ffn2_norm[l], ffn2_w_gate[l], ffn2_w_up[l], ffn2_w_down[l])
        hs = swiglu_half(hs, ffn2_norm[l], ffn2_w_gate[l], ffn2_w_up[l], ffn2_w_down[l])

        k_p_rows.append(kp)
        v_p_rows.append(vp)
        k_s_rows.append(ks_new)
        v_s_rows.append(vs_new)

    y_prompt = rmsnorm(hp, final_norm)
    y_sample = rmsnorm(hs, final_norm)
    k_prompt = jnp.stack(k_p_rows, axis=1)
    v_prompt = jnp.stack(v_p_rows, axis=1)
    k_sample = jnp.stack(k_s_rows, axis=1)
    v_sample = jnp.stack(v_s_rows, axis=1)
    return (y_prompt, y_sample, k_prompt, v_prompt, k_sample, v_sample)
```

```python
import functools
import math

import numpy as np
import jax
import jax.numpy as jnp
from jax import lax
from jax.experimental import pallas as pl
from jax.experimental.pallas import tpu as pltpu

F32 = jnp.float32
BF16 = jnp.bfloat16
NEG_INF = float("-inf")

D_A = 64
D_B = 64
MOBA_BLOCK = 256
MOBA_TOPK = 3
N_BUCKETS = 32
MAX_DISTANCE = 128
EPS = 1e-6
QK_SCALE = D_A ** -0.5

LANES = 128
SUBLANES = 8
VMEM_LIMIT_BYTES = 56 * 1024 * 1024

ATT_TILE = MOBA_BLOCK
FFN_COL_TILE = 512
PROJ_COL_TILE = 512
SAMPLE_DIFF_PAGES = 8
SAMPLE_SCORE_PAGES = 16
SAMPLE_VALUE_PAGES = 8


def _bucket_thresholds():
    n = np.arange(0, 4 * MAX_DISTANCE, dtype=np.int32)
    max_exact = N_BUCKETS // 2
    nf = np.maximum(n, max_exact).astype(np.float32)
    log_b = max_exact + (np.log(nf / np.float32(max_exact)) / np.float32(math.log(MAX_DISTANCE / max_exact))
                         * np.float32(N_BUCKETS - max_exact)).astype(np.int32)
    bucket = np.where(n < max_exact, n, np.minimum(log_b, N_BUCKETS - 1))
    assert bucket[-1] == N_BUCKETS - 1
    return [int(np.argmax(bucket >= b)) for b in range(1, N_BUCKETS)]


BUCKET_THR = _bucket_thresholds()
FAR_DISTANCE = BUCKET_THR[-1]


def _params(*sem):
    return pltpu.CompilerParams(dimension_semantics=sem, vmem_limit_bytes=VMEM_LIMIT_BYTES)


def _rms(x):
    return x * lax.rsqrt(jnp.mean(x * x, axis=-1, keepdims=True) + EPS)


def _dot(a, b):
    return jnp.dot(a, b, preferred_element_type=F32)


def _dot_nt(a, b, precision=None):
    return lax.dot_general(a, b, (((1,), (1,)), ((), ())), preferred_element_type=F32, precision=precision)


def _row_tile(n, pref):
    t = min(n, pref)
    assert n % t == 0
    return t


def _ffn_kernel(x_ref, g_ref, wg_ref, wu_ref, wd_ref, *rest, n_col, final):
    if final:
        fg_ref, o_ref, hn_ref = rest
    else:
        o_ref, hn_ref = rest
    j = pl.program_id(1)

    @pl.when(j == 0)
    def _():
        hn_ref[...] = (_rms(x_ref[...]) * g_ref[...]).astype(BF16)
        o_ref[...] = jnp.zeros_like(o_ref)

    hn = hn_ref[...]
    a = _dot(hn, wg_ref[...])
    b = _dot(hn, wu_ref[...])
    act = (a * jax.nn.sigmoid(a) * b).astype(BF16)
    o_ref[...] += _dot(act, wd_ref[...])

    @pl.when(j == n_col - 1)
    def _():
        h = x_ref[...] + 0.5 * o_ref[...]
        if final:
            h = _rms(h) * fg_ref[...]
        o_ref[...] = h


def _ffn(x, g, wg, wu, wd, final_g=None):
    t, d = x.shape
    f = wg.shape[1]
    tm = _row_tile(t, 512)
    tf = FFN_COL_TILE
    n_col = f // tf
    final = final_g is not None
    in_specs = [
        pl.BlockSpec((tm, d), lambda i, j: (i, 0)),
        pl.BlockSpec((1, d), lambda i, j: (0, 0)),
        pl.BlockSpec((d, tf), lambda i, j: (0, j)),
        pl.BlockSpec((d, tf), lambda i, j: (0, j)),
        pl.BlockSpec((tf, d), lambda i, j: (j, 0)),
    ]
    args = [x, g, wg, wu, wd]
    if final:
        in_specs.append(pl.BlockSpec((1, d), lambda i, j: (0, 0)))
        args.append(final_g)
    return pl.pallas_call(
        functools.partial(_ffn_kernel, n_col=n_col, final=final),
        out_shape=jax.ShapeDtypeStruct((t, d), F32),
        grid=(t // tm, n_col),
        in_specs=in_specs,
        out_specs=pl.BlockSpec((tm, d), lambda i, j: (i, 0)),
        scratch_shapes=[pltpu.VMEM((tm, d), BF16)],
        compiler_params=_params("parallel", "arbitrary"),
        name="ffn_final" if final else "ffn",
    )(*args)


def _proj_kernel(h_ref, g_ref, wq_ref, wk_ref, wv_ref, wa_ref, wb_ref,
                 q_ref, k_ref, v_ref, kb_ref, vb_ref, sa_ref, sb_ref, hn_ref):
    @pl.when(pl.program_id(1) == 0)
    def _():
        hn_ref[...] = (_rms(h_ref[...]) * g_ref[...]).astype(BF16)

    hn = hn_ref[...]
    q_ref[...] = (_dot(hn, wq_ref[...]) * QK_SCALE).astype(BF16)
    k = _dot(hn, wk_ref[...])
    k_ref[...] = k
    kb_ref[...] = k.astype(BF16)
    v = _dot(hn, wv_ref[...])
    v_ref[...] = v
    vb_ref[...] = v.astype(BF16)
    sa_ref[...] = jax.nn.sigmoid(_dot(hn, wa_ref[...])).astype(BF16)
    sb_ref[...] = jax.nn.sigmoid(_dot(hn, wb_ref[...])).astype(BF16)


def _project(h, g, w_in, kv_width):
    t, d = h.shape
    assert kv_width == d and w_in.shape[1] == 5 * d
    tm = _row_tile(t, 512)
    tn = _row_tile(d, PROJ_COL_TILE)
    n_col = d // tn
    w_specs = [pl.BlockSpec((d, tn), functools.partial(lambda i, j, c: (0, c * n_col + j), c=c))
               for c in range(5)]
    out_spec = pl.BlockSpec((tm, tn), lambda i, j: (i, j))
    out_shape = [jax.ShapeDtypeStruct((t, d), dt) for dt in (BF16, F32, F32, BF16, BF16, BF16, BF16)]
    return pl.pallas_call(
        _proj_kernel,
        out_shape=out_shape,
        grid=(t // tm, n_col),
        in_specs=[pl.BlockSpec((tm, d), lambda i, j: (i, 0)),
                  pl.BlockSpec((1, d), lambda i, j: (0, 0))] + w_specs,
        out_specs=[out_spec] * 7,
        scratch_shapes=[pltpu.VMEM((tm, d), BF16)],
        compiler_params=_params("parallel", "arbitrary"),
        name="project",
    )(h, g, w_in, w_in, w_in, w_in, w_in)


def _bias_of_distance(d, tab_ref, col):
    v = jnp.full(d.shape, tab_ref[0, col], F32)
    for b in range(1, N_BUCKETS):
        if b < N_BUCKETS - 1 and BUCKET_THR[b - 1] == BUCKET_THR[b]:
            continue
        v = jnp.where(d >= BUCKET_THR[b - 1], tab_ref[b, col], v)
    return v


def _prompt_bias_kernel(tab_ref, o_ref, *, tile):
    col = pl.program_id(0)
    i = lax.broadcasted_iota(jnp.int32, (tile, tile), 0)
    j = lax.broadcasted_iota(jnp.int32, (tile, tile), 1)
    o_ref[0] = _bias_of_distance(i - j, tab_ref, col)
    o_ref[1] = _bias_of_distance(i - j + tile, tab_ref, col)


def _prompt_bias_tiles(rel_bias, tile):
    n_col = rel_bias.shape[1]
    return pl.pallas_call(
        functools.partial(_prompt_bias_kernel, tile=tile),
        out_shape=jax.ShapeDtypeStruct((n_col, 2, tile, tile), F32),
        grid=(n_col,),
        in_specs=[pl.BlockSpec(memory_space=pltpu.SMEM)],
        out_specs=pl.BlockSpec((None, 2, tile, tile), lambda c: (c, 0, 0, 0)),
        compiler_params=_params("parallel"),
        name="prompt_bias_tiles",
    )(rel_bias)


def _sample_bias_kernel(tab_ref, o_ref, *, n_groups, col_of_group, dec, last_width):
    width = o_ref.shape[-1]
    i = lax.broadcasted_iota(jnp.int32, (dec, width), 0)
    j = lax.broadcasted_iota(jnp.int32, (dec, width), 1)
    for g in range(n_groups):
        col = col_of_group(g)
        rows = slice(g * dec, (g + 1) * dec)
        o_ref[0, rows, :] = _bias_of_distance(last_width + i - j, tab_ref, col)
        o_ref[1, rows, :] = jnp.full((dec, width), tab_ref[N_BUCKETS - 1, col], F32)
        new = _bias_of_distance(i - j, tab_ref, col)
        o_ref[2, rows, :] = jnp.where((j <= i) & (j < dec), new, NEG_INF)


def _sample_bias_tiles(rel_bias, n_groups, col_of_group, dec, last_width):
    width = max(last_width, LANES)
    return pl.pallas_call(
        functools.partial(_sample_bias_kernel, n_groups=n_groups, col_of_group=col_of_group, dec=dec,
                          last_width=last_width),
        out_shape=jax.ShapeDtypeStruct((3, n_groups * dec, width), F32),
        in_specs=[pl.BlockSpec(memory_space=pltpu.SMEM)],
        name="sample_bias_tiles",
    )(rel_bias)


def _lambda_value(lq1_ref, lk1_ref, lq2_ref, lk2_ref, lam_init):
    e1 = jnp.exp(jnp.sum(lq1_ref[...] * lk1_ref[...], axis=-1, keepdims=True))
    e2 = jnp.exp(jnp.sum(lq2_ref[...] * lk2_ref[...], axis=-1, keepdims=True))
    return e1 - e2 + lam_init


def _topk_block_mask(gate, k):
    n = gate.shape[-1]
    lane = lax.broadcasted_iota(jnp.int32, gate.shape, 1).astype(F32)
    mask = jnp.full(gate.shape, NEG_INF, F32)
    for _ in range(k):
        mx = jnp.max(gate, axis=-1, keepdims=True)
        first = jnp.min(jnp.where(gate == mx, lane, float(n)), axis=-1, keepdims=True)
        hit = lane == first
        mask = jnp.where(hit & (mx > NEG_INF), 0.0, mask)
        gate = jnp.where(hit, NEG_INF, gate)
    return mask


def _prompt_attn_kernel(*refs, mode, tile, lam_init):
    if mode == "diff":
        (tab_ref, q_ref, k_ref, v_ref, b1_ref, b2_ref, lq1_ref, lk1_ref, lq2_ref, lk2_ref, gain_ref,
         o_ref, m_ref, l_ref, acc_ref) = refs
    else:
        (tab_ref, q_ref, k_ref, v_ref, b1_ref, b2_ref, kmean_ref,
         o_ref, m_ref, l_ref, acc_ref, sel_ref) = refs
    g = pl.program_id(1)
    qi = pl.program_id(2)
    half = LANES // 2
    lane = lax.broadcasted_iota(jnp.int32, (1, LANES), 1)
    q = q_ref[...]
    q_maps = (jnp.where(lane < half, q, jnp.zeros_like(q)), jnp.where(lane >= half, q, jnp.zeros_like(q)))
    bias_refs = (b1_ref, b2_ref)
    if mode == "diff":
        n_heads = pl.num_programs(1)
        cols = (g, n_heads + g)
    else:
        n_diff_cols = tab_ref.shape[1] - 2 * pl.num_programs(1)
        cols = (n_diff_cols + 2 * g, n_diff_cols + 2 * g + 1)

    if mode == "moba":
        n_blk = kmean_ref.shape[0]
        blk = lax.broadcasted_iota(jnp.int32, (tile, n_blk), 1)
        kmean = kmean_ref[...]
        kmean_maps = (jnp.where(lane < half, kmean, 0.0), jnp.where(lane >= half, kmean, 0.0))
        for m in range(2):
            gate = _dot_nt(q.astype(F32), kmean_maps[m], precision=lax.Precision.HIGHEST)
            gate = jnp.where(blk < qi, gate, NEG_INF)
            sel_ref[m] = _topk_block_mask(gate, min(MOBA_TOPK, n_blk))

    kd = k_ref[pl.ds(pl.multiple_of(qi * tile, tile), tile), :]
    vd = v_ref[pl.ds(pl.multiple_of(qi * tile, tile), tile), :]
    row = lax.broadcasted_iota(jnp.int32, (tile, tile), 0)
    colv = lax.broadcasted_iota(jnp.int32, (tile, tile), 1)
    causal = row >= colv
    for m in range(2):
        s = _dot_nt(q_maps[m], kd) + bias_refs[m][0]
        s = jnp.where(causal, s, NEG_INF)
        mx = jnp.max(s, axis=-1, keepdims=True)
        p = jnp.exp(s - mx)
        m_ref[m] = mx
        l_ref[m] = jnp.sum(p, axis=-1, keepdims=True)
        acc_ref[m] = _dot(p.astype(BF16), vd)

    def past_tile(kj, bias_of_map):
        start = pl.multiple_of(kj * tile, tile)
        kt = k_ref[pl.ds(start, tile), :]
        vt = v_ref[pl.ds(start, tile), :]
        for m in range(2):
            s = _dot_nt(q_maps[m], kt) + bias_of_map(m)
            if mode == "moba":
                blk_lane = lax.broadcasted_iota(jnp.int32, sel_ref.shape[1:], 1)
                s = s + jnp.max(jnp.where(blk_lane == kj, sel_ref[m], NEG_INF), axis=-1, keepdims=True)
            m_old = m_ref[m]
            mx = jnp.maximum(m_old, jnp.max(s, axis=-1, keepdims=True))
            alpha = jnp.exp(m_old - mx)
            p = jnp.exp(s - mx)
            l_ref[m] = alpha * l_ref[m] + jnp.sum(p, axis=-1, keepdims=True)
            acc_ref[m] = alpha * acc_ref[m] + _dot(p.astype(BF16), vt)
            m_ref[m] = mx

    @pl.when(qi >= 1)
    def _():
        past_tile(qi - 1, lambda m: bias_refs[m][1])

    def far_body(kj, carry):
        past_tile(kj, lambda m: tab_ref[N_BUCKETS - 1, cols[m]])
        return carry

    lax.fori_loop(0, jnp.maximum(qi - 1, 0), far_body, 0)

    o1 = acc_ref[0] / l_ref[0]
    o2 = acc_ref[1] / l_ref[1]
    if mode == "diff":
        lam = _lambda_value(lq1_ref, lk1_ref, lq2_ref, lk2_ref, lam_init)
        o = o1 - lam * o2
        o_ref[...] = (_rms(o) * gain_ref[...] * (1.0 - lam_init)).astype(o_ref.dtype)
    else:
        o_ref[...] = jnp.where(lane < half, o1, o2).astype(o_ref.dtype)


def _prompt_attention(mode, rel_bias, bias_tiles, q, kb, vb, batch, seq, width_a, extras, lam_init=0.0):
    tile = ATT_TILE
    n_q = seq // tile
    n_heads_a = width_a // (2 * D_A)
    if mode == "diff":
        n_groups = width_a // LANES
        col0 = 0
        cols = (lambda g: g, lambda g: n_heads_a + g)
    else:
        n_groups = (q.shape[1] - width_a) // LANES
        col0 = width_a // LANES
        cols = (lambda g: 2 * n_heads_a + 2 * g, lambda g: 2 * n_heads_a + 2 * g + 1)
    in_specs = [
        pl.BlockSpec(memory_space=pltpu.SMEM),
        pl.BlockSpec((tile, LANES), lambda b, g, i: (b * n_q + i, col0 + g)),
        pl.BlockSpec((seq, LANES), lambda b, g, i: (b, col0 + g)),
        pl.BlockSpec((seq, LANES), lambda b, g, i: (b, col0 + g)),
        pl.BlockSpec((None, 2, tile, tile), lambda b, g, i: (cols[0](g), 0, 0, 0)),
        pl.BlockSpec((None, 2, tile, tile), lambda b, g, i: (cols[1](g), 0, 0, 0)),
    ]
    args = [rel_bias, q, kb, vb, bias_tiles, bias_tiles]
    scratch = [pltpu.VMEM((2, tile, 1), F32), pltpu.VMEM((2, tile, 1), F32), pltpu.VMEM((2, tile, LANES), F32)]
    if mode == "diff":
        lq1, lk1, lq2, lk2, gain = extras
        in_specs += [pl.BlockSpec((1, D_A), lambda b, g, i: (0, 0))] * 4
        in_specs += [pl.BlockSpec((1, LANES), lambda b, g, i: (0, 0))]
        args += [lq1, lk1, lq2, lk2, gain]
    else:
        (kmean,) = extras
        n_blk = kmean.shape[1]
        in_specs += [pl.BlockSpec((None, n_blk, LANES), lambda b, g, i: (b, 0, g))]
        args += [kmean]
        scratch += [pltpu.VMEM((2, tile, n_blk), F32)]
    return pl.pallas_call(
        functools.partial(_prompt_attn_kernel, mode=mode, tile=tile, lam_init=lam_init),
        out_shape=jax.ShapeDtypeStruct((batch * seq, n_groups * LANES), BF16),
        grid=(batch, n_groups, n_q),
        in_specs=in_specs,
        out_specs=pl.BlockSpec((tile, LANES), lambda b, g, i: (b * n_q + i, g)),
        scratch_shapes=scratch,
        compiler_params=_params("parallel", "parallel", "arbitrary"),
        name="prompt_" + mode,
    )(*args)


def _block_mean_kernel(k_ref, o_ref):
    o_ref[...] = jnp.mean(k_ref[...], axis=0, keepdims=True)


def _prompt_block_means(k, width_a, width_b):
    assert width_a % width_b == 0
    n_blk = k.shape[0] // MOBA_BLOCK
    return pl.pallas_call(
        _block_mean_kernel,
        out_shape=jax.ShapeDtypeStruct((n_blk, 1, width_b), F32),
        grid=(n_blk,),
        in_specs=[pl.BlockSpec((MOBA_BLOCK, width_b), lambda i: (i, width_a // width_b))],
        out_specs=pl.BlockSpec((None, 1, width_b), lambda i: (i, 0, 0)),
        compiler_params=_params("parallel"),
        name="prompt_block_means",
    )(k)


def _block_diag_queries(q, n_groups, group_width):
    lane = lax.broadcasted_iota(jnp.int32, q.shape, 1)
    rows = [jnp.where((lane >= g * group_width) & (lane < (g + 1) * group_width), q, jnp.zeros_like(q))
            for g in range(n_groups)]
    return jnp.concatenate(rows, axis=0)


def _gather_block_diag(x, n_groups, rows_per_group, group_width):
    lane = lax.broadcasted_iota(jnp.int32, (rows_per_group, x.shape[1]), 1)
    out = jnp.zeros((rows_per_group, x.shape[1]), x.dtype)
    for g in range(n_groups):
        part = x[g * rows_per_group:(g + 1) * rows_per_group, :]
        out = jnp.where((lane >= g * group_width) & (lane < (g + 1) * group_width), part, out)
    return out


def _pad_rows(x, n_rows):
    return jnp.concatenate([x, jnp.zeros((n_rows - x.shape[0], x.shape[1]), x.dtype)], axis=0)


def _softmax_start(s, v, m_ref, l_ref, acc_ref):
    mx = jnp.max(s, axis=-1, keepdims=True)
    p = jnp.exp(s - mx)
    m_ref[...] = mx
    l_ref[...] = jnp.sum(p, axis=-1, keepdims=True)
    acc_ref[...] = _dot(p.astype(BF16), v)


def _softmax_step(s, v, m_ref, l_ref, acc_ref):
    m_old = m_ref[...]
    mx = jnp.maximum(m_old, jnp.max(s, axis=-1, keepdims=True))
    alpha = jnp.exp(m_old - mx)
    p = jnp.exp(s - mx)
    l_ref[...] = alpha * l_ref[...] + jnp.sum(p, axis=-1, keepdims=True)
    acc_ref[...] = alpha * acc_ref[...] + _dot(p.astype(BF16), v)
    m_ref[...] = mx


def _sample_diff_kernel(pt_ref, *refs, n_pages_step, page, dec, n_heads, lam_init):
    k_refs = refs[:n_pages_step]
    v_refs = refs[n_pages_step:2 * n_pages_step]
    (q_ref, kn_ref, vn_ref, bias_ref, lq1_ref, lk1_ref, lq2_ref, lk2_ref, gain_ref,
     o_ref, qb_ref, m_ref, l_ref, acc_ref) = refs[2 * n_pages_step:]
    c = pl.program_id(1)
    n_steps = pl.num_programs(1)
    n_rows = qb_ref.shape[0]

    @pl.when(c == 0)
    def _():
        qb_ref[...] = _block_diag_queries(q_ref[...].astype(F32), 2 * n_heads, D_A).astype(BF16)
        k_new = _pad_rows(kn_ref[...], LANES).astype(BF16)
        v_new = _pad_rows(vn_ref[...], LANES).astype(BF16)
        s = _dot_nt(qb_ref[...], k_new) + bias_ref[2]
        _softmax_start(s, v_new, m_ref, l_ref, acc_ref)

    qb = qb_ref[...]
    scores = []
    for p in range(n_pages_step):
        s = _dot_nt(qb, k_refs[p][...].astype(BF16))
        if p == n_pages_step - 1:
            s = s + jnp.where(c == n_steps - 1, bias_ref[0], bias_ref[1])
        else:
            s = s + bias_ref[1]
        scores.append(s)
    s = jnp.concatenate(scores, axis=1)
    v = jnp.concatenate([v_refs[p][...].astype(BF16) for p in range(n_pages_step)], axis=0)
    _softmax_step(s, v, m_ref, l_ref, acc_ref)

    @pl.when(c == n_steps - 1)
    def _():
        on = acc_ref[...] / l_ref[...]
        lane = lax.broadcasted_iota(jnp.int32, (dec, on.shape[1]), 1)
        o1 = jnp.zeros((dec, on.shape[1]), F32)
        o2 = jnp.zeros((dec, on.shape[1]), F32)
        for h in range(n_heads):
            in_head = (lane >= h * 2 * D_A) & (lane < (h + 1) * 2 * D_A)
            o1 = jnp.where(in_head, on[(2 * h) * dec:(2 * h + 1) * dec, :], o1)
            o2 = jnp.where(in_head, on[(2 * h + 1) * dec:(2 * h + 2) * dec, :], o2)
        lam = _lambda_value(lq1_ref, lk1_ref, lq2_ref, lk2_ref, lam_init)
        o = o1 - lam * o2
        for h in range(n_heads):
            cols = slice(h * 2 * D_A, (h + 1) * 2 * D_A)
            o_ref[:, cols] = (_rms(o[:, cols]) * gain_ref[...] * (1.0 - lam_init)).astype(o_ref.dtype)


def _sample_diff(page_table, cache_k, cache_v, layer, q, k_new, v_new, bias, lam_vecs, gain, width_a, lam_init):
    n_seq, n_pages = page_table.shape
    page = cache_k.shape[2]
    dec = q.shape[1]
    n_heads = width_a // (2 * D_A)
    pps = SAMPLE_DIFF_PAGES if n_pages % SAMPLE_DIFF_PAGES == 0 else 1
    n_steps = n_pages // pps
    n_rows = 2 * n_heads * dec
    assert page == LANES and 2 * D_A == LANES and bias.shape == (3, n_rows, LANES)

    def page_spec(p):
        return pl.BlockSpec((None, None, page, width_a),
                            lambda b, c, pt: (pt[b * n_pages + c * pps + p], layer, 0, 0))

    row_spec = pl.BlockSpec((None, dec, width_a), lambda b, c, pt: (b, 0, 0))
    vec_spec = pl.BlockSpec((1, D_A), lambda b, c, pt: (0, 0))
    in_specs = ([page_spec(p) for p in range(pps)] * 2
                + [row_spec, row_spec, row_spec,
                   pl.BlockSpec((3, n_rows, LANES), lambda b, c, pt: (0, 0, 0)),
                   vec_spec, vec_spec, vec_spec, vec_spec,
                   pl.BlockSpec((1, LANES), lambda b, c, pt: (0, 0))])
    return pl.pallas_call(
        functools.partial(_sample_diff_kernel, n_pages_step=pps, page=page, dec=dec, n_heads=n_heads,
                          lam_init=lam_init),
        out_shape=jax.ShapeDtypeStruct((n_seq, dec, width_a), BF16),
        grid_spec=pltpu.PrefetchScalarGridSpec(
            num_scalar_prefetch=1,
            grid=(n_seq, n_steps),
            in_specs=in_specs,
            out_specs=pl.BlockSpec((None, dec, width_a), lambda b, c, pt: (b, 0, 0)),
            scratch_shapes=[pltpu.VMEM((n_rows, width_a), BF16), pltpu.VMEM((n_rows, 1), F32),
                            pltpu.VMEM((n_rows, 1), F32), pltpu.VMEM((n_rows, width_a), F32)]),
        compiler_params=_params("parallel", "arbitrary"),
        name="sample_diff",
    )(page_table.reshape(-1), *([cache_k] * pps), *([cache_v] * pps), q, k_new, v_new, bias, *lam_vecs, gain)


def _sample_score_kernel(pt_ref, *refs, n_pages_step, dec, n_heads, pages_per_block):
    k_refs = refs[:n_pages_step]
    q_ref, s_ref, kmean_ref, qb_ref = refs[n_pages_step:]

    @pl.when(pl.program_id(1) == 0)
    def _():
        qb_ref[...] = _block_diag_queries(q_ref[...].astype(F32), n_heads, D_B).astype(BF16)

    qb = qb_ref[...]
    page = k_refs[0].shape[0]
    for p in range(n_pages_step):
        s_ref[:, p * page:(p + 1) * page] = _dot_nt(qb, k_refs[p][...].astype(BF16))
    for blk in range(n_pages_step // pages_per_block):
        total = k_refs[blk * pages_per_block][...]
        for p in range(1, pages_per_block):
            total = total + k_refs[blk * pages_per_block + p][...]
        kmean_ref[blk:blk + 1, :] = jnp.sum(total, axis=0, keepdims=True) * (1.0 / MOBA_BLOCK)


def _sample_scores(page_table, cache_k, layer, q, width_a, width_b):
    n_seq, n_pages = page_table.shape
    page = cache_k.shape[2]
    dec = q.shape[1]
    n_heads = width_b // D_B
    pps = SAMPLE_SCORE_PAGES
    ppb = MOBA_BLOCK // page
    assert n_pages % pps == 0 and pps // ppb == SUBLANES and width_a % width_b == 0
    n_steps = n_pages // pps
    n_rows = n_heads * dec
    col_blk = width_a // width_b

    def page_spec(p):
        return pl.BlockSpec((None, None, page, width_b),
                            lambda b, c, pt: (pt[b * n_pages + c * pps + p], layer, 0, col_blk))

    return pl.pallas_call(
        functools.partial(_sample_score_kernel, n_pages_step=pps, dec=dec, n_heads=n_heads, pages_per_block=ppb),
        out_shape=[jax.ShapeDtypeStruct((n_seq, n_rows, n_pages * page), F32),
                   jax.ShapeDtypeStruct((n_seq, n_pages // ppb, width_b), F32)],
        grid_spec=pltpu.PrefetchScalarGridSpec(
            num_scalar_prefetch=1,
            grid=(n_seq, n_steps),
            in_specs=[page_spec(p) for p in range(pps)]
            + [pl.BlockSpec((None, dec, width_b), lambda b, c, pt: (b, 0, col_blk))],
            out_specs=[pl.BlockSpec((None, n_rows, pps * page), lambda b, c, pt: (b, 0, c)),
                       pl.BlockSpec((None, pps // ppb, width_b), lambda b, c, pt: (b, c, 0))],
            scratch_shapes=[pltpu.VMEM((n_rows, width_b), BF16)]),
        compiler_params=_params("parallel", "arbitrary"),
        name="sample_moba_scores",
    )(page_table.reshape(-1), *([cache_k] * pps), q)


def _sample_moba_kernel(pt_ref, *refs, n_pages_step, dec, n_heads, n_blk):
    v_refs = refs[:n_pages_step]
    (q_ref, kn_ref, vn_ref, s_ref, kmean_ref, bias_ref,
     o_ref, sel_ref, m_ref, l_ref, acc_ref) = refs[n_pages_step:]
    c = pl.program_id(1)
    n_steps = pl.num_programs(1)
    page = v_refs[0].shape[0]
    blocks_step = n_pages_step * page // MOBA_BLOCK

    @pl.when(c == 0)
    def _():
        qf = _block_diag_queries(q_ref[...].astype(F32), n_heads, D_B)
        gate = _dot_nt(qf, kmean_ref[...], precision=lax.Precision.HIGHEST)
        sel_ref[...] = _topk_block_mask(gate, min(MOBA_TOPK, n_blk))
        k_new = _pad_rows(kn_ref[...], LANES).astype(BF16)
        v_new = _pad_rows(vn_ref[...], LANES).astype(BF16)
        s = _dot_nt(qf.astype(BF16), k_new) + bias_ref[2][:, :LANES]
        _softmax_start(s, v_new, m_ref, l_ref, acc_ref)

    sel = sel_ref[...]
    blk_lane = lax.broadcasted_iota(jnp.int32, sel.shape, 1)
    parts = []
    for blk in range(blocks_step):
        s = s_ref[:, blk * MOBA_BLOCK:(blk + 1) * MOBA_BLOCK]
        far = bias_ref[1][:, :1]
        if blk == blocks_step - 1:
            s = s + jnp.where(c == n_steps - 1, bias_ref[0], far)
        else:
            s = s + far
        picked = jnp.max(jnp.where(blk_lane == c * blocks_step + blk, sel, NEG_INF), axis=-1, keepdims=True)
        parts.append(s + picked)
    s = jnp.concatenate(parts, axis=1)
    v = jnp.concatenate([v_refs[p][...].astype(BF16) for p in range(n_pages_step)], axis=0)
    _softmax_step(s, v, m_ref, l_ref, acc_ref)

    @pl.when(c == n_steps - 1)
    def _():
        on = acc_ref[...] / l_ref[...]
        o_ref[...] = _gather_block_diag(on, n_heads, dec, D_B).astype(o_ref.dtype)


def _sample_moba(page_table, cache_v, layer, q, k_new, v_new, scores, kmean, bias, width_a, width_b):
    n_seq, n_pages = page_table.shape
    page = cache_v.shape[2]
    dec = q.shape[1]
    n_heads = width_b // D_B
    n_blk = kmean.shape[1]
    pps = SAMPLE_VALUE_PAGES
    assert n_pages % pps == 0 and (pps * page) % MOBA_BLOCK == 0
    n_steps = n_pages // pps
    n_rows = n_heads * dec
    col_blk = width_a // width_b
    assert bias.shape == (3, n_rows, MOBA_BLOCK)

    def page_spec(p):
        return pl.BlockSpec((None, None, page, width_b),
                            lambda b, c, pt: (pt[b * n_pages + c * pps + p], layer, 0, col_blk))

    row_spec = pl.BlockSpec((None, dec, width_b), lambda b, c, pt: (b, 0, col_blk))
    return pl.pallas_call(
        functools.partial(_sample_moba_kernel, n_pages_step=pps, dec=dec, n_heads=n_heads, n_blk=n_blk),
        out_shape=jax.ShapeDtypeStruct((n_seq, dec, width_b), BF16),
        grid_spec=pltpu.PrefetchScalarGridSpec(
            num_scalar_prefetch=1,
            grid=(n_seq, n_steps),
            in_specs=[page_spec(p) for p in range(pps)]
            + [row_spec, row_spec, row_spec,
               pl.BlockSpec((None, n_rows, pps * page), lambda b, c, pt: (b, 0, c)),
               pl.BlockSpec((None, n_blk, width_b), lambda b, c, pt: (b, 0, 0)),
               pl.BlockSpec((3, n_rows, MOBA_BLOCK), lambda b, c, pt: (0, 0, 0))],
            out_specs=pl.BlockSpec((None, dec, width_b), lambda b, c, pt: (b, 0, 0)),
            scratch_shapes=[pltpu.VMEM((n_rows, n_blk), F32), pltpu.VMEM((n_rows, 1), F32),
                            pltpu.VMEM((n_rows, 1), F32), pltpu.VMEM((n_rows, width_b), F32)]),
        compiler_params=_params("parallel", "arbitrary"),
        name="sample_moba_values",
    )(page_table.reshape(-1), *([cache_v] * pps), q, k_new, v_new, scores, kmean, bias)


def _merge_kernel(oa_ref, ob_ref, sa_ref, sb_ref, h_ref, wa_ref, wb_ref, wo_ref, o_ref):
    m = (sa_ref[...].astype(F32) * _dot(oa_ref[...], wa_ref[...])
         + sb_ref[...].astype(F32) * _dot(ob_ref[...], wb_ref[...]))
    o_ref[...] = h_ref[...] + _dot(m.astype(BF16), wo_ref[...])


def _merge(oa, ob, sa, sb, h, w_up_a, w_up_b, w_out):
    t, d = h.shape
    tm = _row_tile(t, 256)
    wa, wb = oa.shape[1], ob.shape[1]
    row = lambda w: pl.BlockSpec((tm, w), lambda i: (i, 0))
    full = lambda a: pl.BlockSpec(a.shape, lambda i: (0, 0))
    return pl.pallas_call(
        _merge_kernel,
        out_shape=jax.ShapeDtypeStruct((t, d), F32),
        grid=(t // tm,),
        in_specs=[row(wa), row(wb), row(d), row(d), row(d), full(w_up_a), full(w_up_b), full(w_out)],
        out_specs=row(d),
        compiler_params=_params("parallel"),
        name="merge",
    )(oa, ob, sa, sb, h, w_up_a, w_up_b, w_out)


def _pad_cols(w, mult):
    pad = (-w.shape[1]) % mult
    return jnp.pad(w, ((0, 0), (0, pad)))


def _pad_rows_to(w, mult):
    pad = (-w.shape[0]) % mult
    return jnp.pad(w, ((0, pad), (0, 0)))


def kernel(x_prompt, x_sample, cache_k, cache_v, page_table, rel_bias, ffn1_norm, ffn1_w_gate, ffn1_w_up,
           ffn1_w_down, mix_norm, w_in, lambda_q1, lambda_k1, lambda_q2, lambda_k2, subln_gain, w_up_a,
           w_up_b, w_out, ffn2_norm, ffn2_w_gate, ffn2_w_up, ffn2_w_down, final_norm):
    batch, seq, d_model = x_prompt.shape
    n_seq, dec, _ = x_sample.shape
    depth = w_in.shape[0]
    width_a = w_up_a.shape[1]
    width_b = w_up_b.shape[1]
    kv_width = width_a + width_b
    n_heads_a = width_a // (2 * D_A)
    n_heads_b = width_b // D_B
    page = cache_k.shape[2]
    assert seq % ATT_TILE == 0 and ATT_TILE >= FAR_DISTANCE and page >= FAR_DISTANCE
    assert rel_bias.shape == (N_BUCKETS, 2 * n_heads_a + n_heads_b)

    hp = x_prompt.reshape(batch * seq, d_model)
    hs = x_sample.reshape(n_seq * dec, d_model)
    row = lambda a: a.reshape(1, -1)

    prompt_tiles = _prompt_bias_tiles(rel_bias, ATT_TILE)
    diff_tiles = _sample_bias_tiles(rel_bias, 2 * n_heads_a, lambda g: (g % 2) * n_heads_a + g // 2, dec, page)
    moba_tiles = _sample_bias_tiles(rel_bias, n_heads_b, lambda g: 2 * n_heads_a + g, dec, MOBA_BLOCK)

    k_p, v_p, k_s, v_s = [], [], [], []
    for l in range(depth):
        lam_init = 0.8 - 0.6 * math.exp(-0.3 * l)
        lam_vecs = [row(a[l]) for a in (lambda_q1, lambda_k1, lambda_q2, lambda_k2)]
        gain = row(subln_gain[l])

        def ffn_weights(wg, wu, wd):
            return (_pad_cols(wg[l].astype(BF16), FFN_COL_TILE), _pad_cols(wu[l].astype(BF16), FFN_COL_TILE),
                    _pad_rows_to(wd[l].astype(BF16), FFN_COL_TILE))

        w1 = ffn_weights(ffn1_w_gate, ffn1_w_up, ffn1_w_down)
        hp = _ffn(hp, row(ffn1_norm[l]), *w1)
        hs = _ffn(hs, row(ffn1_norm[l]), *w1)

        w_in_b = w_in[l].astype(BF16)
        qp, kp, vp, kpb, vpb, sap, sbp = _project(hp, row(mix_norm[l]), w_in_b, kv_width)
        qs, ks, vs, _, _, sas, sbs = _project(hs, row(mix_norm[l]), w_in_b, kv_width)

        oa_p = _prompt_attention("diff", rel_bias, prompt_tiles, qp, kpb, vpb, batch, seq, width_a,
                                 lam_vecs + [gain], lam_init)
        kmean_p = _prompt_block_means(kp, width_a, width_b).reshape(batch, seq // MOBA_BLOCK, width_b)
        ob_p = _prompt_attention("moba", rel_bias, prompt_tiles, qp, kpb, vpb, batch, seq, width_a, [kmean_p])

        qs3 = qs.reshape(n_seq, dec, kv_width)
        ks3 = ks.reshape(n_seq, dec, kv_width)
        vs3 = vs.reshape(n_seq, dec, kv_width)
        oa_s = _sample_diff(page_table, cache_k, cache_v, l, qs3, ks3, vs3, diff_tiles, lam_vecs, gain,
                            width_a, lam_init)
        scores, kmean_s = _sample_scores(page_table, cache_k, l, qs3, width_a, width_b)
        ob_s = _sample_moba(page_table, cache_v, l, qs3, ks3, vs3, scores, kmean_s, moba_tiles, width_a, width_b)

        wa_b, wb_b, wo_b = w_up_a[l].astype(BF16), w_up_b[l].astype(BF16), w_out[l].astype(BF16)
        hp = _merge(oa_p, ob_p, sap, sbp, hp, wa_b, wb_b, wo_b)
        hs = _merge(oa_s.reshape(n_seq * dec, width_a), ob_s.reshape(n_seq * dec, width_b), sas, sbs, hs,
                    wa_b, wb_b, wo_b)

        w2 = ffn_weights(ffn2_w_gate, ffn2_w_up, ffn2_w_down)
        last = l == depth - 1
        fg = row(final_norm) if last else None
        hp = _ffn(hp, row(ffn2_norm[l]), *w2, final_g=fg)
        hs = _ffn(hs, row(ffn2_norm[l]), *w2, final_g=fg)

        k_p.append(kp.reshape(batch, seq, kv_width))
        v_p.append(vp.reshape(batch, seq, kv_width))
        k_s.append(ks3)
        v_s.append(vs3)

    return (hp.reshape(batch, seq, d_model), hs.reshape(n_seq, dec, d_model),
            jnp.stack(k_p, axis=1), jnp.stack(v_p, axis=1), jnp.stack(k_s, axis=1), jnp.stack(v_s, axis=1))
```

```python
import functools
import math

import numpy as np
import jax
import jax.numpy as jnp
from jax import lax
from jax.experimental import pallas as pl
from jax.experimental.pallas import tpu as pltpu

F32 = jnp.float32
BF16 = jnp.bfloat16
NEG_INF = float("-inf")

D_A = 64
D_B = 64
MOBA_BLOCK = 256
MOBA_TOPK = 3
N_BUCKETS = 32
MAX_DISTANCE = 128
EPS = 1e-6
QK_SCALE = D_A ** -0.5

LANES = 128
SUBLANES = 8
VMEM_LIMIT_BYTES = 56 * 1024 * 1024

ATT_TILE = MOBA_BLOCK
FFN_COL_TILE = 512
PROJ_COL_TILE = 512
SAMPLE_DIFF_PAGES = 8
SAMPLE_MOBA_PAGES = 16


def _bucket_thresholds():
    n = np.arange(0, 4 * MAX_DISTANCE, dtype=np.int32)
    max_exact = N_BUCKETS // 2
    nf = np.maximum(n, max_exact).astype(np.float32)
    log_b = max_exact + (np.log(nf / np.float32(max_exact)) / np.float32(math.log(MAX_DISTANCE / max_exact))
                         * np.float32(N_BUCKETS - max_exact)).astype(np.int32)
    bucket = np.where(n < max_exact, n, np.minimum(log_b, N_BUCKETS - 1))
    assert bucket[-1] == N_BUCKETS - 1
    return [int(np.argmax(bucket >= b)) for b in range(1, N_BUCKETS)]


BUCKET_THR = _bucket_thresholds()
FAR_DISTANCE = BUCKET_THR[-1]


def _params(*sem):
    return pltpu.CompilerParams(dimension_semantics=sem, vmem_limit_bytes=VMEM_LIMIT_BYTES)


def _rms(x):
    return x * lax.rsqrt(jnp.mean(x * x, axis=-1, keepdims=True) + EPS)


def _dot(a, b):
    return jnp.dot(a, b, preferred_element_type=F32)


def _dot_nt(a, b, precision=None):
    return lax.dot_general(a, b, (((1,), (1,)), ((), ())), preferred_element_type=F32, precision=precision)


def _row_tile(n, pref):
    t = min(n, pref)
    assert n % t == 0
    return t


def _ffn_kernel(x_ref, g_ref, wg_ref, wu_ref, wd_ref, *rest, n_col, final):
    if final:
        fg_ref, o_ref, hn_ref = rest
    else:
        o_ref, hn_ref = rest
    j = pl.program_id(1)

    @pl.when(j == 0)
    def _():
        hn_ref[...] = (_rms(x_ref[...]) * g_ref[...]).astype(BF16)
        o_ref[...] = jnp.zeros_like(o_ref)

    hn = hn_ref[...]
    a = _dot(hn, wg_ref[...])
    b = _dot(hn, wu_ref[...])
    act = (a * jax.nn.sigmoid(a) * b).astype(BF16)
    o_ref[...] += _dot(act, wd_ref[...])

    @pl.when(j == n_col - 1)
    def _():
        h = x_ref[...] + 0.5 * o_ref[...]
        if final:
            h = _rms(h) * fg_ref[...]
        o_ref[...] = h


def _ffn(x, g, wg, wu, wd, final_g=None):
    t, d = x.shape
    f = wg.shape[1]
    tm = _row_tile(t, 512)
    tf = FFN_COL_TILE
    n_col = f // tf
    final = final_g is not None
    in_specs = [
        pl.BlockSpec((tm, d), lambda i, j: (i, 0)),
        pl.BlockSpec((1, d), lambda i, j: (0, 0)),
        pl.BlockSpec((d, tf), lambda i, j: (0, j)),
        pl.BlockSpec((d, tf), lambda i, j: (0, j)),
        pl.BlockSpec((tf, d), lambda i, j: (j, 0)),
    ]
    args = [x, g, wg, wu, wd]
    if final:
        in_specs.append(pl.BlockSpec((1, d), lambda i, j: (0, 0)))
        args.append(final_g)
    return pl.pallas_call(
        functools.partial(_ffn_kernel, n_col=n_col, final=final),
        out_shape=jax.ShapeDtypeStruct((t, d), F32),
        grid=(t // tm, n_col),
        in_specs=in_specs,
        out_specs=pl.BlockSpec((tm, d), lambda i, j: (i, 0)),
        scratch_shapes=[pltpu.VMEM((tm, d), BF16)],
        compiler_params=_params("parallel", "arbitrary"),
        name="ffn_final" if final else "ffn",
    )(*args)


def _proj_kernel(h_ref, g_ref, wq_ref, wk_ref, wv_ref, wa_ref, wb_ref,
                 q_ref, k_ref, v_ref, kb_ref, vb_ref, sa_ref, sb_ref, hn_ref):
    @pl.when(pl.program_id(1) == 0)
    def _():
        hn_ref[...] = (_rms(h_ref[...]) * g_ref[...]).astype(BF16)

    hn = hn_ref[...]
    q_ref[...] = (_dot(hn, wq_ref[...]) * QK_SCALE).astype(BF16)
    k = _dot(hn, wk_ref[...])
    k_ref[...] = k
    kb_ref[...] = k.astype(BF16)
    v = _dot(hn, wv_ref[...])
    v_ref[...] = v
    vb_ref[...] = v.astype(BF16)
    sa_ref[...] = jax.nn.sigmoid(_dot(hn, wa_ref[...])).astype(BF16)
    sb_ref[...] = jax.nn.sigmoid(_dot(hn, wb_ref[...])).astype(BF16)


def _project(h, g, w_in, kv_width):
    t, d = h.shape
    assert kv_width == d and w_in.shape[1] == 5 * d
    tm = _row_tile(t, 512)
    tn = _row_tile(d, PROJ_COL_TILE)
    n_col = d // tn
    w_specs = [pl.BlockSpec((d, tn), functools.partial(lambda i, j, c: (0, c * n_col + j), c=c))
               for c in range(5)]
    out_spec = pl.BlockSpec((tm, tn), lambda i, j: (i, j))
    out_shape = [jax.ShapeDtypeStruct((t, d), dt) for dt in (BF16, F32, F32, BF16, BF16, BF16, BF16)]
    return pl.pallas_call(
        _proj_kernel,
        out_shape=out_shape,
        grid=(t // tm, n_col),
        in_specs=[pl.BlockSpec((tm, d), lambda i, j: (i, 0)),
                  pl.BlockSpec((1, d), lambda i, j: (0, 0))] + w_specs,
        out_specs=[out_spec] * 7,
        scratch_shapes=[pltpu.VMEM((tm, d), BF16)],
        compiler_params=_params("parallel", "arbitrary"),
        name="project",
    )(h, g, w_in, w_in, w_in, w_in, w_in)


def _bias_of_distance(d, tab_ref, col):
    v = jnp.full(d.shape, tab_ref[0, col], F32)
    for b in range(1, N_BUCKETS):
        if b < N_BUCKETS - 1 and BUCKET_THR[b - 1] == BUCKET_THR[b]:
            continue
        v = jnp.where(d >= BUCKET_THR[b - 1], tab_ref[b, col], v)
    return v


def _prompt_bias_kernel(tab_ref, o_ref, *, tile):
    col = pl.program_id(0)
    i = lax.broadcasted_iota(jnp.int32, (tile, tile), 0)
    j = lax.broadcasted_iota(jnp.int32, (tile, tile), 1)
    o_ref[0] = _bias_of_distance(i - j, tab_ref, col)
    o_ref[1] = _bias_of_distance(i - j + tile, tab_ref, col)


def _prompt_bias_tiles(rel_bias, tile):
    n_col = rel_bias.shape[1]
    return pl.pallas_call(
        functools.partial(_prompt_bias_kernel, tile=tile),
        out_shape=jax.ShapeDtypeStruct((n_col, 2, tile, tile), F32),
        grid=(n_col,),
        in_specs=[pl.BlockSpec(memory_space=pltpu.SMEM)],
        out_specs=pl.BlockSpec((None, 2, tile, tile), lambda c: (c, 0, 0, 0)),
        compiler_params=_params("parallel"),
        name="prompt_bias_tiles",
    )(rel_bias)


def _sample_bias_kernel(tab_ref, o_ref, *, n_groups, col_of_group, dec, last_width):
    width = o_ref.shape[-1]
    i = lax.broadcasted_iota(jnp.int32, (dec, width), 0)
    j = lax.broadcasted_iota(jnp.int32, (dec, width), 1)
    for g in range(n_groups):
        col = col_of_group(g)
        rows = slice(g * dec, (g + 1) * dec)
        o_ref[0, rows, :] = _bias_of_distance(last_width + i - j, tab_ref, col)
        o_ref[1, rows, :] = jnp.full((dec, width), tab_ref[N_BUCKETS - 1, col], F32)
        new = _bias_of_distance(i - j, tab_ref, col)
        o_ref[2, rows, :] = jnp.where((j <= i) & (j < dec), new, NEG_INF)


def _sample_bias_tiles(rel_bias, n_groups, col_of_group, dec, last_width):
    width = max(last_width, LANES)
    return pl.pallas_call(
        functools.partial(_sample_bias_kernel, n_groups=n_groups, col_of_group=col_of_group, dec=dec,
                          last_width=last_width),
        out_shape=jax.ShapeDtypeStruct((3, n_groups * dec, width), F32),
        in_specs=[pl.BlockSpec(memory_space=pltpu.SMEM)],
        name="sample_bias_tiles",
    )(rel_bias)


def _lambda_value(lq1_ref, lk1_ref, lq2_ref, lk2_ref, lam_init):
    e1 = jnp.exp(jnp.sum(lq1_ref[...] * lk1_ref[...], axis=-1, keepdims=True))
    e2 = jnp.exp(jnp.sum(lq2_ref[...] * lk2_ref[...], axis=-1, keepdims=True))
    return e1 - e2 + lam_init


def _topk_block_mask(gate, k):
    n = gate.shape[-1]
    lane = lax.broadcasted_iota(jnp.int32, gate.shape, 1).astype(F32)
    mask = jnp.full(gate.shape, NEG_INF, F32)
    for _ in range(k):
        mx = jnp.max(gate, axis=-1, keepdims=True)
        first = jnp.min(jnp.where(gate == mx, lane, float(n)), axis=-1, keepdims=True)
        hit = lane == first
        mask = jnp.where(hit & (mx > NEG_INF), 0.0, mask)
        gate = jnp.where(hit, NEG_INF, gate)
    return mask


def _bf16_split_rows(mats, n_pad):
    parts = []
    for x in mats:
        rest = x
        for _ in range(3):
            piece = rest.astype(BF16).astype(F32)
            parts.append(_pad_rows(piece, n_pad))
            rest = rest - piece
    return jnp.concatenate(parts, axis=0).astype(BF16)


def _topk_rank_mask(gate, n, k):
    row_id = lax.broadcasted_iota(jnp.int32, gate.shape, 0)
    rank = jnp.zeros(gate.shape, F32)
    for other in range(n):
        cand = gate[other:other + 1, :]
        wins_tie = jnp.where(row_id > other, 1.0, 0.0)
        rank = rank + jnp.where(cand > gate, 1.0, jnp.where(cand == gate, wins_tie, 0.0))
    return jnp.where(rank < float(k), jnp.where(gate > NEG_INF, 0.0, NEG_INF), NEG_INF)


def _prompt_attn_kernel(*refs, mode, tile, n_q, n_heads_a, lam_init):
    if mode == "diff":
        (tab_ref, q_ref, k_ref, v_ref, b1_ref, b2_ref, lq1_ref, lk1_ref, lq2_ref, lk2_ref, gain_ref,
         o_ref, s_ref, p_ref) = refs
    else:
        (tab_ref, q_ref, k_ref, v_ref, b1_ref, b2_ref, kmean_ref, o_ref, s_ref, p_ref) = refs
    g = pl.program_id(1)
    half = LANES // 2
    lane = lax.broadcasted_iota(jnp.int32, (1, LANES), 1)
    bias_refs = (b1_ref, b2_ref)
    if mode == "diff":
        cols = (g, n_heads_a + g)
        lam = _lambda_value(lq1_ref, lk1_ref, lq2_ref, lk2_ref, lam_init)
    else:
        cols = (2 * n_heads_a + 2 * g, 2 * n_heads_a + 2 * g + 1)
        n_blk = kmean_ref.shape[0]
        n_pad = -(-n_blk // SUBLANES) * SUBLANES
        assert 2 * n_pad <= LANES
        kmean = kmean_ref[...]
        kmean_maps = (jnp.where(lane < half, kmean, 0.0), jnp.where(lane >= half, kmean, 0.0))
        gates = _dot_nt(_bf16_split_rows(kmean_maps, n_pad), q_ref[...])
        blk_id = lax.broadcasted_iota(jnp.int32, (n_pad, n_q * tile), 0)
        q_blk = lax.broadcasted_iota(jnp.int32, (n_pad, n_q * tile), 1) // tile
        sel_t = []
        for m in range(2):
            gate = gates[3 * m * n_pad:(3 * m + 1) * n_pad] + gates[(3 * m + 1) * n_pad:(3 * m + 2) * n_pad] \
                + gates[(3 * m + 2) * n_pad:(3 * m + 3) * n_pad]
            gate = jnp.where(blk_id < q_blk, gate, NEG_INF)
            sel_t.append(_topk_rank_mask(gate, n_blk, min(MOBA_TOPK, n_blk)))
    far = [tab_ref[N_BUCKETS - 1, cols[m]] for m in range(2)]
    row = lax.broadcasted_iota(jnp.int32, (tile, tile), 0)
    colv = lax.broadcasted_iota(jnp.int32, (tile, tile), 1)
    causal = row >= colv

    for qi in range(n_q):
        q = q_ref[qi * tile:(qi + 1) * tile, :]
        q_maps = (jnp.where(lane < half, q, jnp.zeros_like(q)), jnp.where(lane >= half, q, jnp.zeros_like(q)))
        if mode == "moba" and qi > 0:
            packed = [sel_t[m][:, qi * tile:(qi + 1) * tile] for m in range(2)]
            packed.append(jnp.zeros((LANES - 2 * n_pad, tile), F32))
            sel_q = jnp.concatenate(packed, axis=0).T
        outs = []
        for m in range(2):
            run_max = None
            for j in range(qi + 1):
                s = _dot_nt(q_maps[m], k_ref[j * tile:(j + 1) * tile, :])
                if j == qi:
                    s = jnp.where(causal, s + bias_refs[m][0], NEG_INF)
                else:
                    shift = bias_refs[m][1] if j == qi - 1 else far[m]
                    if mode == "moba":
                        shift = shift + sel_q[:, m * n_pad + j:m * n_pad + j + 1]
                    s = s + shift
                s_ref[m, j] = s
                run_max = s if run_max is None else jnp.maximum(run_max, s)
            mx = jnp.max(run_max, axis=-1, keepdims=True)
            run_sum = None
            for j in range(qi + 1):
                p = jnp.exp(s_ref[m, j] - mx)
                run_sum = p if run_sum is None else run_sum + p
                p_ref[m, :, j * tile:(j + 1) * tile] = p.astype(BF16)
            n_keys = (qi + 1) * tile
            denom = jnp.sum(run_sum, axis=-1, keepdims=True)
            outs.append(_dot(p_ref[m, :, :n_keys], v_ref[:n_keys, :]) / denom)
        rows = slice(qi * tile, (qi + 1) * tile)
        if mode == "diff":
            o = outs[0] - lam * outs[1]
            o_ref[rows, :] = (_rms(o) * gain_ref[...] * (1.0 - lam_init)).astype(o_ref.dtype)
        else:
            o_ref[rows, :] = jnp.where(lane < half, outs[0], outs[1]).astype(o_ref.dtype)


def _prompt_attention(mode, rel_bias, bias_tiles, q, kb, vb, batch, seq, width_a, extras, lam_init=0.0):
    tile = ATT_TILE
    n_q = seq // tile
    n_heads_a = width_a // (2 * D_A)
    if mode == "diff":
        n_groups = width_a // LANES
        col0 = 0
        cols = (lambda g: g, lambda g: n_heads_a + g)
    else:
        n_groups = (q.shape[1] - width_a) // LANES
        col0 = width_a // LANES
        cols = (lambda g: 2 * n_heads_a + 2 * g, lambda g: 2 * n_heads_a + 2 * g + 1)
    seq_spec = pl.BlockSpec((seq, LANES), lambda b, g: (b, col0 + g))
    in_specs = [
        pl.BlockSpec(memory_space=pltpu.SMEM),
        seq_spec, seq_spec, seq_spec,
        pl.BlockSpec((None, 2, tile, tile), lambda b, g: (cols[0](g), 0, 0, 0)),
        pl.BlockSpec((None, 2, tile, tile), lambda b, g: (cols[1](g), 0, 0, 0)),
    ]
    args = [rel_bias, q, kb, vb, bias_tiles, bias_tiles]
    if mode == "diff":
        lq1, lk1, lq2, lk2, gain = extras
        in_specs += [pl.BlockSpec((1, D_A), lambda b, g: (0, 0))] * 4
        in_specs += [pl.BlockSpec((1, LANES), lambda b, g: (0, 0))]
        args += [lq1, lk1, lq2, lk2, gain]
    else:
        (kmean,) = extras
        n_blk = kmean.shape[1]
        in_specs += [pl.BlockSpec((None, n_blk, LANES), lambda b, g: (b, 0, g))]
        args += [kmean]
    return pl.pallas_call(
        functools.partial(_prompt_attn_kernel, mode=mode, tile=tile, n_q=n_q, n_heads_a=n_heads_a,
                          lam_init=lam_init),
        out_shape=jax.ShapeDtypeStruct((batch * seq, n_groups * LANES), BF16),
        grid=(batch, n_groups),
        in_specs=in_specs,
        out_specs=pl.BlockSpec((seq, LANES), lambda b, g: (b, g)),
        scratch_shapes=[pltpu.VMEM((2, n_q, tile, tile), F32), pltpu.VMEM((2, tile, seq), BF16)],
        compiler_params=_params("parallel", "parallel"),
        name="prompt_" + mode,
    )(*args)


def _block_mean_kernel(k_ref, o_ref):
    o_ref[...] = jnp.mean(k_ref[...], axis=0, keepdims=True)


def _prompt_block_means(k, width_a, width_b):
    assert width_a % width_b == 0
    n_blk = k.shape[0] // MOBA_BLOCK
    return pl.pallas_call(
        _block_mean_kernel,
        out_shape=jax.ShapeDtypeStruct((n_blk, 1, width_b), F32),
        grid=(n_blk,),
        in_specs=[pl.BlockSpec((MOBA_BLOCK, width_b), lambda i: (i, width_a // width_b))],
        out_specs=pl.BlockSpec((None, 1, width_b), lambda i: (i, 0, 0)),
        compiler_params=_params("parallel"),
        name="prompt_block_means",
    )(k)


def _block_diag_queries(q, n_groups, group_width):
    lane = lax.broadcasted_iota(jnp.int32, q.shape, 1)
    rows = [jnp.where((lane >= g * group_width) & (lane < (g + 1) * group_width), q, jnp.zeros_like(q))
            for g in range(n_groups)]
    return jnp.concatenate(rows, axis=0)


def _gather_block_diag(x, n_groups, rows_per_group, group_width):
    lane = lax.broadcasted_iota(jnp.int32, (rows_per_group, x.shape[1]), 1)
    out = jnp.zeros((rows_per_group, x.shape[1]), x.dtype)
    for g in range(n_groups):
        part = x[g * rows_per_group:(g + 1) * rows_per_group, :]
        out = jnp.where((lane >= g * group_width) & (lane < (g + 1) * group_width), part, out)
    return out


def _pad_rows(x, n_rows):
    if x.shape[0] == n_rows:
        return x
    return jnp.concatenate([x, jnp.zeros((n_rows - x.shape[0], x.shape[1]), x.dtype)], axis=0)


def _softmax_start(s, v, m_ref, l_ref, acc_ref):
    mx = jnp.max(s, axis=-1, keepdims=True)
    p = jnp.exp(s - mx)
    m_ref[...] = mx
    l_ref[...] = jnp.sum(p, axis=-1, keepdims=True)
    acc_ref[...] = _dot(p.astype(BF16), v)


def _softmax_step(s, v, m_ref, l_ref, acc_ref):
    m_old = m_ref[...]
    mx = jnp.maximum(m_old, jnp.max(s, axis=-1, keepdims=True))
    alpha = jnp.exp(m_old - mx)
    p = jnp.exp(s - mx)
    l_ref[...] = alpha * l_ref[...] + jnp.sum(p, axis=-1, keepdims=True)
    acc_ref[...] = alpha * acc_ref[...] + _dot(p.astype(BF16), v)
    m_ref[...] = mx


def _sample_diff_kernel(pt_ref, *refs, n_pages_step, page, dec, n_heads, lam_init):
    k_refs = refs[:n_pages_step]
    v_refs = refs[n_pages_step:2 * n_pages_step]
    (q_ref, kn_ref, vn_ref, bias_ref, lq1_ref, lk1_ref, lq2_ref, lk2_ref, gain_ref,
     o_ref, qb_ref, m_ref, l_ref, acc_ref) = refs[2 * n_pages_step:]
    c = pl.program_id(1)
    n_steps = pl.num_programs(1)

    @pl.when(c == 0)
    def _():
        qb_ref[...] = _block_diag_queries(q_ref[...].astype(F32), 2 * n_heads, D_A).astype(BF16)
        k_new = _pad_rows(kn_ref[...], LANES).astype(BF16)
        v_new = _pad_rows(vn_ref[...], LANES).astype(BF16)
        s = _dot_nt(qb_ref[...], k_new) + bias_ref[2]
        _softmax_start(s, v_new, m_ref, l_ref, acc_ref)

    qb = qb_ref[...]
    scores = []
    for p in range(n_pages_step):
        s = _dot_nt(qb, k_refs[p][...].astype(BF16))
        if p == n_pages_step - 1:
            s = s + jnp.where(c == n_steps - 1, bias_ref[0], bias_ref[1])
        else:
            s = s + bias_ref[1]
        scores.append(s)
    s = jnp.concatenate(scores, axis=1)
    v = jnp.concatenate([v_refs[p][...].astype(BF16) for p in range(n_pages_step)], axis=0)
    _softmax_step(s, v, m_ref, l_ref, acc_ref)

    @pl.when(c == n_steps - 1)
    def _():
        on = acc_ref[...] / l_ref[...]
        lane = lax.broadcasted_iota(jnp.int32, (dec, on.shape[1]), 1)
        o1 = jnp.zeros((dec, on.shape[1]), F32)
        o2 = jnp.zeros((dec, on.shape[1]), F32)
        for h in range(n_heads):
            in_head = (lane >= h * 2 * D_A) & (lane < (h + 1) * 2 * D_A)
            o1 = jnp.where(in_head, on[(2 * h) * dec:(2 * h + 1) * dec, :], o1)
            o2 = jnp.where(in_head, on[(2 * h + 1) * dec:(2 * h + 2) * dec, :], o2)
        lam = _lambda_value(lq1_ref, lk1_ref, lq2_ref, lk2_ref, lam_init)
        o = o1 - lam * o2
        for h in range(n_heads):
            cols = slice(h * 2 * D_A, (h + 1) * 2 * D_A)
            o_ref[:, cols] = (_rms(o[:, cols]) * gain_ref[...] * (1.0 - lam_init)).astype(o_ref.dtype)


def _sample_diff(page_table, cache_k, cache_v, layer, q, k_new, v_new, bias, lam_vecs, gain, width_a, lam_init):
    n_seq, n_pages = page_table.shape
    page = cache_k.shape[2]
    dec = q.shape[1]
    n_heads = width_a // (2 * D_A)
    pps = SAMPLE_DIFF_PAGES if n_pages % SAMPLE_DIFF_PAGES == 0 else 1
    n_steps = n_pages // pps
    n_rows = 2 * n_heads * dec
    assert page == LANES and 2 * D_A == LANES and bias.shape == (3, n_rows, LANES)

    def page_spec(p):
        return pl.BlockSpec((None, None, page, width_a),
                            lambda b, c, pt: (pt[b * n_pages + c * pps + p], layer, 0, 0))

    row_spec = pl.BlockSpec((None, dec, width_a), lambda b, c, pt: (b, 0, 0))
    vec_spec = pl.BlockSpec((1, D_A), lambda b, c, pt: (0, 0))
    in_specs = ([page_spec(p) for p in range(pps)] * 2
                + [row_spec, row_spec, row_spec,
                   pl.BlockSpec((3, n_rows, LANES), lambda b, c, pt: (0, 0, 0)),
                   vec_spec, vec_spec, vec_spec, vec_spec,
                   pl.BlockSpec((1, LANES), lambda b, c, pt: (0, 0))])
    return pl.pallas_call(
        functools.partial(_sample_diff_kernel, n_pages_step=pps, page=page, dec=dec, n_heads=n_heads,
                          lam_init=lam_init),
        out_shape=jax.ShapeDtypeStruct((n_seq, dec, width_a), BF16),
        grid_spec=pltpu.PrefetchScalarGridSpec(
            num_scalar_prefetch=1,
            grid=(n_seq, n_steps),
            in_specs=in_specs,
            out_specs=pl.BlockSpec((None, dec, width_a), lambda b, c, pt: (b, 0, 0)),
            scratch_shapes=[pltpu.VMEM((n_rows, width_a), BF16), pltpu.VMEM((n_rows, 1), F32),
                            pltpu.VMEM((n_rows, 1), F32), pltpu.VMEM((n_rows, width_a), F32)]),
        compiler_params=_params("parallel", "arbitrary"),
        name="sample_diff",
    )(page_table.reshape(-1), *([cache_k] * pps), *([cache_v] * pps), q, k_new, v_new, bias, *lam_vecs, gain)


def _sample_score_kernel(pt_ref, *refs, n_pages_step, dec, n_heads, pages_per_block):
    k_refs = refs[:n_pages_step]
    q_ref, bias_ref, s_ref, smax_ref, kmean_ref, qb_ref = refs[n_pages_step:]
    c = pl.program_id(1)
    n_steps = pl.num_programs(1)
    blocks_step = n_pages_step // pages_per_block

    @pl.when(c == 0)
    def _():
        qb_ref[...] = _block_diag_queries(q_ref[...].astype(F32), n_heads, D_B).astype(BF16)

    qb = qb_ref[...]
    page = k_refs[0].shape[0]
    far = bias_ref[1][:, :page]
    blk_lane = lax.broadcasted_iota(jnp.int32, smax_ref.shape, 1)
    smax = jnp.full(smax_ref.shape, NEG_INF, F32)
    for blk in range(blocks_step):
        blk_max = None
        for p in range(blk * pages_per_block, (blk + 1) * pages_per_block):
            s = _dot_nt(qb, k_refs[p][...].astype(BF16))
            in_last_block = p - (n_pages_step - pages_per_block)
            if in_last_block >= 0:
                near = bias_ref[0][:, in_last_block * page:(in_last_block + 1) * page]
                s = s + jnp.where(c == n_steps - 1, near, far)
            else:
                s = s + far
            s_ref[:, p * page:(p + 1) * page] = s
            blk_max = s if blk_max is None else jnp.maximum(blk_max, s)
        smax = jnp.where(blk_lane == c * blocks_step + blk, jnp.max(blk_max, axis=-1, keepdims=True), smax)
    smax_ref[...] = smax
    for blk in range(blocks_step):
        total = k_refs[blk * pages_per_block][...]
        for p in range(1, pages_per_block):
            total = total + k_refs[blk * pages_per_block + p][...]
        kmean_ref[blk:blk + 1, :] = jnp.sum(total, axis=0, keepdims=True) * (1.0 / MOBA_BLOCK)


def _sample_scores(page_table, cache_k, layer, q, bias, width_a, width_b):
    n_seq, n_pages = page_table.shape
    page = cache_k.shape[2]
    dec = q.shape[1]
    n_heads = width_b // D_B
    pps = SAMPLE_MOBA_PAGES
    ppb = MOBA_BLOCK // page
    assert n_pages % pps == 0 and pps // ppb == SUBLANES and width_a % width_b == 0
    n_steps = n_pages // pps
    n_rows = n_heads * dec
    col_blk = width_a // width_b
    assert bias.shape == (3, n_rows, MOBA_BLOCK) and n_pages // ppb <= LANES

    def page_spec(p):
        return pl.BlockSpec((None, None, page, width_b),
                            lambda b, c, pt: (pt[b * n_pages + c * pps + p], layer, 0, col_blk))

    return pl.pallas_call(
        functools.partial(_sample_score_kernel, n_pages_step=pps, dec=dec, n_heads=n_heads, pages_per_block=ppb),
        out_shape=[jax.ShapeDtypeStruct((n_seq, n_steps, n_rows, pps * page), F32),
                   jax.ShapeDtypeStruct((n_seq, n_steps, n_rows, LANES), F32),
                   jax.ShapeDtypeStruct((n_seq, n_pages // ppb, width_b), F32)],
        grid_spec=pltpu.PrefetchScalarGridSpec(
            num_scalar_prefetch=1,
            grid=(n_seq, n_steps),
            in_specs=[page_spec(p) for p in range(pps)]
            + [pl.BlockSpec((None, dec, width_b), lambda b, c, pt: (b, 0, col_blk)),
               pl.BlockSpec((3, n_rows, MOBA_BLOCK), lambda b, c, pt: (0, 0, 0))],
            out_specs=[pl.BlockSpec((None, None, n_rows, pps * page), lambda b, c, pt: (b, c, 0, 0)),
                       pl.BlockSpec((None, None, n_rows, LANES), lambda b, c, pt: (b, c, 0, 0)),
                       pl.BlockSpec((None, pps // ppb, width_b), lambda b, c, pt: (b, c, 0))],
            scratch_shapes=[pltpu.VMEM((n_rows, width_b), BF16)]),
        compiler_params=_params("parallel", "arbitrary"),
        name="sample_moba_scores",
    )(page_table.reshape(-1), *([cache_k] * pps), q, bias)


def _sample_moba_kernel(pt_ref, *refs, n_pages_step, dec, n_heads, n_blk):
    v_refs = refs[:n_pages_step]
    (q_ref, kn_ref, vn_ref, s_ref, smax_ref, kmean_ref, bias_ref,
     o_ref, sel_ref, m_ref, l_ref, acc_ref) = refs[n_pages_step:]
    c = pl.program_id(1)
    n_steps = s_ref.shape[0]
    page = v_refs[0].shape[0]
    blocks_step = n_pages_step * page // MOBA_BLOCK

    @pl.when(c == 0)
    def _():
        qb = _block_diag_queries(q_ref[...].astype(F32), n_heads, D_B).astype(BF16)
        parts = _dot_nt(qb, _bf16_split_rows([kmean_ref[...]], LANES))
        gate = parts[:, :LANES] + parts[:, LANES:2 * LANES] + parts[:, 2 * LANES:]
        blk_lane = lax.broadcasted_iota(jnp.int32, gate.shape, 1)
        gate = jnp.where(blk_lane < n_blk, gate, NEG_INF)
        sel = _topk_block_mask(gate, min(MOBA_TOPK, n_blk))
        sel_ref[...] = sel
        k_new = _pad_rows(kn_ref[...], LANES).astype(BF16)
        v_new = _pad_rows(vn_ref[...], LANES).astype(BF16)
        s_new = _dot_nt(qb, k_new) + bias_ref[2][:, :LANES]
        smax = smax_ref[0]
        for step in range(1, n_steps):
            smax = jnp.maximum(smax, smax_ref[step])
        mx = jnp.maximum(jnp.max(s_new, axis=-1, keepdims=True), jnp.max(smax + sel, axis=-1, keepdims=True))
        p = jnp.exp(s_new - mx)
        m_ref[...] = mx
        l_ref[...] = jnp.sum(p, axis=-1, keepdims=True)
        acc_ref[...] = _dot(p.astype(BF16), v_new)

    sel = sel_ref[...]
    blk_lane = lax.broadcasted_iota(jnp.int32, sel.shape, 1)
    mx = m_ref[...]
    parts = []
    run_sum = None
    for blk in range(blocks_step):
        picked = jnp.max(jnp.where(blk_lane == c * blocks_step + blk, sel, NEG_INF), axis=-1, keepdims=True)
        p = jnp.exp(s_ref[c, :, blk * MOBA_BLOCK:(blk + 1) * MOBA_BLOCK] + (picked - mx))
        run_sum = p if run_sum is None else run_sum + p
        parts.append(p.astype(BF16))
    p = jnp.concatenate(parts, axis=1)
    v = jnp.concatenate([v_refs[i][...].astype(BF16) for i in range(n_pages_step)], axis=0)
    l_ref[...] += jnp.sum(run_sum, axis=-1, keepdims=True)
    acc_ref[...] += _dot(p, v)

    @pl.when(c == n_steps - 1)
    def _():
        on = acc_ref[...] / l_ref[...]
        o_ref[...] = _gather_block_diag(on, n_heads, dec, D_B).astype(o_ref.dtype)


def _sample_moba(page_table, cache_v, layer, q, k_new, v_new, scores, score_max, kmean, bias, width_a, width_b):
    n_seq, n_pages = page_table.shape
    page = cache_v.shape[2]
    dec = q.shape[1]
    n_heads = width_b // D_B
    n_blk = kmean.shape[1]
    pps = SAMPLE_MOBA_PAGES
    n_steps = n_pages // pps
    n_rows = n_heads * dec
    col_blk = width_a // width_b
    assert scores.shape == (n_seq, n_steps, n_rows, pps * page) and bias.shape == (3, n_rows, MOBA_BLOCK)

    def page_spec(p):
        return pl.BlockSpec((None, None, page, width_b),
                            lambda b, c, pt: (pt[b * n_pages + c * pps + p], layer, 0, col_blk))

    row_spec = pl.BlockSpec((None, dec, width_b), lambda b, c, pt: (b, 0, col_blk))
    return pl.pallas_call(
        functools.partial(_sample_moba_kernel, n_pages_step=pps, dec=dec, n_heads=n_heads, n_blk=n_blk),
        out_shape=jax.ShapeDtypeStruct((n_seq, dec, width_b), BF16),
        grid_spec=pltpu.PrefetchScalarGridSpec(
            num_scalar_prefetch=1,
            grid=(n_seq, n_steps),
            in_specs=[page_spec(p) for p in range(pps)]
            + [row_spec, row_spec, row_spec,
               pl.BlockSpec((None, n_steps, n_rows, pps * page), lambda b, c, pt: (b, 0, 0, 0)),
               pl.BlockSpec((None, n_steps, n_rows, LANES), lambda b, c, pt: (b, 0, 0, 0)),
               pl.BlockSpec((None, n_blk, width_b), lambda b, c, pt: (b, 0, 0)),
               pl.BlockSpec((3, n_rows, MOBA_BLOCK), lambda b, c, pt: (0, 0, 0))],
            out_specs=pl.BlockSpec((None, dec, width_b), lambda b, c, pt: (b, 0, 0)),
            scratch_shapes=[pltpu.VMEM((n_rows, LANES), F32), pltpu.VMEM((n_rows, 1), F32),
                            pltpu.VMEM((n_rows, 1), F32), pltpu.VMEM((n_rows, width_b), F32)]),
        compiler_params=_params("parallel", "arbitrary"),
        name="sample_moba_values",
    )(page_table.reshape(-1), *([cache_v] * pps), q, k_new, v_new, scores, score_max, kmean, bias)


def _merge_kernel(oa_ref, ob_ref, sa_ref, sb_ref, h_ref, wa_ref, wb_ref, wo_ref, o_ref):
    m = (sa_ref[...].astype(F32) * _dot(oa_ref[...], wa_ref[...])
         + sb_ref[...].astype(F32) * _dot(ob_ref[...], wb_ref[...]))
    o_ref[...] = h_ref[...] + _dot(m.astype(BF16), wo_ref[...])


def _merge(oa, ob, sa, sb, h, w_up_a, w_up_b, w_out):
    t, d = h.shape
    tm = _row_tile(t, 256)
    wa, wb = oa.shape[1], ob.shape[1]
    row = lambda w: pl.BlockSpec((tm, w), lambda i: (i, 0))
    full = lambda a: pl.BlockSpec(a.shape, lambda i: (0, 0))
    return pl.pallas_call(
        _merge_kernel,
        out_shape=jax.ShapeDtypeStruct((t, d), F32),
        grid=(t // tm,),
        in_specs=[row(wa), row(wb), row(d), row(d), row(d), full(w_up_a), full(w_up_b), full(w_out)],
        out_specs=row(d),
        compiler_params=_params("parallel"),
        name="merge",
    )(oa, ob, sa, sb, h, w_up_a, w_up_b, w_out)


def _pad_cols(w, mult):
    pad = (-w.shape[1]) % mult
    return jnp.pad(w, ((0, 0), (0, pad)))


def _pad_rows_to(w, mult):
    pad = (-w.shape[0]) % mult
    return jnp.pad(w, ((0, pad), (0, 0)))


def kernel(x_prompt, x_sample, cache_k, cache_v, page_table, rel_bias, ffn1_norm, ffn1_w_gate, ffn1_w_up,
           ffn1_w_down, mix_norm, w_in, lambda_q1, lambda_k1, lambda_q2, lambda_k2, subln_gain, w_up_a,
           w_up_b, w_out, ffn2_norm, ffn2_w_gate, ffn2_w_up, ffn2_w_down, final_norm):
    batch, seq, d_model = x_prompt.shape
    n_seq, dec, _ = x_sample.shape
    depth = w_in.shape[0]
    width_a = w_up_a.shape[1]
    width_b = w_up_b.shape[1]
    kv_width = width_a + width_b
    n_heads_a = width_a // (2 * D_A)
    n_heads_b = width_b // D_B
    page = cache_k.shape[2]
    assert seq % ATT_TILE == 0 and ATT_TILE >= FAR_DISTANCE and page >= FAR_DISTANCE
    assert rel_bias.shape == (N_BUCKETS, 2 * n_heads_a + n_heads_b)

    hp = x_prompt.reshape(batch * seq, d_model)
    hs = x_sample.reshape(n_seq * dec, d_model)
    row = lambda a: a.reshape(1, -1)

    prompt_tiles = _prompt_bias_tiles(rel_bias, ATT_TILE)
    diff_tiles = _sample_bias_tiles(rel_bias, 2 * n_heads_a, lambda g: (g % 2) * n_heads_a + g // 2, dec, page)
    moba_tiles = _sample_bias_tiles(rel_bias, n_heads_b, lambda g: 2 * n_heads_a + g, dec, MOBA_BLOCK)

    k_p, v_p, k_s, v_s = [], [], [], []
    for l in range(depth):
        lam_init = 0.8 - 0.6 * math.exp(-0.3 * l)
        lam_vecs = [row(a[l]) for a in (lambda_q1, lambda_k1, lambda_q2, lambda_k2)]
        gain = row(subln_gain[l])

        def ffn_weights(wg, wu, wd):
            return (_pad_cols(wg[l].astype(BF16), FFN_COL_TILE), _pad_cols(wu[l].astype(BF16), FFN_COL_TILE),
                    _pad_rows_to(wd[l].astype(BF16), FFN_COL_TILE))

        w1 = ffn_weights(ffn1_w_gate, ffn1_w_up, ffn1_w_down)
        hp = _ffn(hp, row(ffn1_norm[l]), *w1)
        hs = _ffn(hs, row(ffn1_norm[l]), *w1)

        w_in_b = w_in[l].astype(BF16)
        qp, kp, vp, kpb, vpb, sap, sbp = _project(hp, row(mix_norm[l]), w_in_b, kv_width)
        qs, ks, vs, _, _, sas, sbs = _project(hs, row(mix_norm[l]), w_in_b, kv_width)

        oa_p = _prompt_attention("diff", rel_bias, prompt_tiles, qp, kpb, vpb, batch, seq, width_a,
                                 lam_vecs + [gain], lam_init)
        kmean_p = _prompt_block_means(kp, width_a, width_b).reshape(batch, seq // MOBA_BLOCK, width_b)
        ob_p = _prompt_attention("moba", rel_bias, prompt_tiles, qp, kpb, vpb, batch, seq, width_a, [kmean_p])

        qs3 = qs.reshape(n_seq, dec, kv_width)
        ks3 = ks.reshape(n_seq, dec, kv_width)
        vs3 = vs.reshape(n_seq, dec, kv_width)
        oa_s = _sample_diff(page_table, cache_k, cache_v, l, qs3, ks3, vs3, diff_tiles, lam_vecs, gain,
                            width_a, lam_init)
        scores, score_max, kmean_s = _sample_scores(page_table, cache_k, l, qs3, moba_tiles, width_a, width_b)
        ob_s = _sample_moba(page_table, cache_v, l, qs3, ks3, vs3, scores, score_max, kmean_s, moba_tiles,
                            width_a, width_b)

        wa_b, wb_b, wo_b = w_up_a[l].astype(BF16), w_up_b[l].astype(BF16), w_out[l].astype(BF16)
        hp = _merge(oa_p, ob_p, sap, sbp, hp, wa_b, wb_b, wo_b)
        hs = _merge(oa_s.reshape(n_seq * dec, width_a), ob_s.reshape(n_seq * dec, width_b), sas, sbs, hs,
                    wa_b, wb_b, wo_b)

        w2 = ffn_weights(ffn2_w_gate, ffn2_w_up, ffn2_w_down)
        last = l == depth - 1
        fg = row(final_norm) if last else None
        hp = _ffn(hp, row(ffn2_norm[l]), *w2, final_g=fg)
        hs = _ffn(hs, row(ffn2_norm[l]), *w2, final_g=fg)

        k_p.append(kp.reshape(batch, seq, kv_width))
        v_p.append(vp.reshape(batch, seq, kv_width))
        k_s.append(ks3)
        v_s.append(vs3)

    return (hp.reshape(batch, seq, d_model), hs.reshape(n_seq, dec, d_model),
            jnp.stack(k_p, axis=1), jnp.stack(v_p, axis=1), jnp.stack(k_s, axis=1), jnp.stack(v_s, axis=1))
```

```python
import functools
import math

import numpy as np
import jax
import jax.numpy as jnp
from jax import lax
from jax.experimental import pallas as pl
from jax.experimental.pallas import tpu as pltpu

F32 = jnp.float32
BF16 = jnp.bfloat16
NEG_INF = float("-inf")

D_A = 64
D_B = 64
MOBA_BLOCK = 256
MOBA_TOPK = 3
N_BUCKETS = 32
MAX_DISTANCE = 128
EPS = 1e-6
QK_SCALE = D_A ** -0.5

LANES = 128
SUBLANES = 8
VMEM_LIMIT_BYTES = 56 * 1024 * 1024

ATT_TILE = MOBA_BLOCK
FFN_COL_TILE = 512
PROJ_COL_TILE = 512
SAMPLE_DIFF_PAGES = 16
SAMPLE_MOBA_PAGES = 16


def _bucket_thresholds():
    n = np.arange(0, 4 * MAX_DISTANCE, dtype=np.int32)
    max_exact = N_BUCKETS // 2
    nf = np.maximum(n, max_exact).astype(np.float32)
    log_b = max_exact + (np.log(nf / np.float32(max_exact)) / np.float32(math.log(MAX_DISTANCE / max_exact))
                         * np.float32(N_BUCKETS - max_exact)).astype(np.int32)
    bucket = np.where(n < max_exact, n, np.minimum(log_b, N_BUCKETS - 1))
    assert bucket[-1] == N_BUCKETS - 1
    return [int(np.argmax(bucket >= b)) for b in range(1, N_BUCKETS)]


BUCKET_THR = _bucket_thresholds()
FAR_DISTANCE = BUCKET_THR[-1]


def _params(*sem):
    return pltpu.CompilerParams(dimension_semantics=sem, vmem_limit_bytes=VMEM_LIMIT_BYTES)


def _rms(x):
    return x * lax.rsqrt(jnp.mean(x * x, axis=-1, keepdims=True) + EPS)


def _dot(a, b):
    return jnp.dot(a, b, preferred_element_type=F32)


def _dot_nt(a, b, precision=None):
    return lax.dot_general(a, b, (((1,), (1,)), ((), ())), preferred_element_type=F32, precision=precision)


def _row_tile(n, pref):
    t = min(n, pref)
    assert n % t == 0
    return t


def _ffn_kernel(x_ref, g_ref, wg_ref, wu_ref, wd_ref, *rest, n_col, final):
    if final:
        fg_ref, o_ref, hn_ref = rest
    else:
        o_ref, hn_ref = rest
    j = pl.program_id(1)

    @pl.when(j == 0)
    def _():
        hn_ref[...] = (_rms(x_ref[...]) * g_ref[...]).astype(BF16)
        o_ref[...] = jnp.zeros_like(o_ref)

    hn = hn_ref[...]
    a = _dot(hn, wg_ref[...])
    b = _dot(hn, wu_ref[...])
    act = (a * jax.nn.sigmoid(a) * b).astype(BF16)
    o_ref[...] += _dot(act, wd_ref[...])

    @pl.when(j == n_col - 1)
    def _():
        h = x_ref[...] + 0.5 * o_ref[...]
        if final:
            h = _rms(h) * fg_ref[...]
        o_ref[...] = h


def _ffn(x, g, wg, wu, wd, final_g=None):
    t, d = x.shape
    f = wg.shape[1]
    tm = _row_tile(t, 512)
    tf = FFN_COL_TILE
    n_col = f // tf
    final = final_g is not None
    in_specs = [
        pl.BlockSpec((tm, d), lambda i, j: (i, 0)),
        pl.BlockSpec((1, d), lambda i, j: (0, 0)),
        pl.BlockSpec((d, tf), lambda i, j: (0, j)),
        pl.BlockSpec((d, tf), lambda i, j: (0, j)),
        pl.BlockSpec((tf, d), lambda i, j: (j, 0)),
    ]
    args = [x, g, wg, wu, wd]
    if final:
        in_specs.append(pl.BlockSpec((1, d), lambda i, j: (0, 0)))
        args.append(final_g)
    return pl.pallas_call(
        functools.partial(_ffn_kernel, n_col=n_col, final=final),
        out_shape=jax.ShapeDtypeStruct((t, d), F32),
        grid=(t // tm, n_col),
        in_specs=in_specs,
        out_specs=pl.BlockSpec((tm, d), lambda i, j: (i, 0)),
        scratch_shapes=[pltpu.VMEM((tm, d), BF16)],
        compiler_params=_params("parallel", "arbitrary"),
        name="ffn_final" if final else "ffn",
    )(*args)


def _proj_kernel(h_ref, g_ref, wq_ref, wk_ref, wv_ref, wa_ref, wb_ref,
                 q_ref, k_ref, v_ref, kb_ref, vb_ref, sa_ref, sb_ref, hn_ref):
    @pl.when(pl.program_id(1) == 0)
    def _():
        hn_ref[...] = (_rms(h_ref[...]) * g_ref[...]).astype(BF16)

    hn = hn_ref[...]
    q_ref[...] = (_dot(hn, wq_ref[...]) * QK_SCALE).astype(BF16)
    k = _dot(hn, wk_ref[...])
    k_ref[...] = k
    kb_ref[...] = k.astype(BF16)
    v = _dot(hn, wv_ref[...])
    v_ref[...] = v
    vb_ref[...] = v.astype(BF16)
    sa_ref[...] = jax.nn.sigmoid(_dot(hn, wa_ref[...])).astype(BF16)
    sb_ref[...] = jax.nn.sigmoid(_dot(hn, wb_ref[...])).astype(BF16)


def _project(h, g, w_in, kv_width):
    t, d = h.shape
    assert kv_width == d and w_in.shape[1] == 5 * d
    tm = _row_tile(t, 512)
    tn = _row_tile(d, PROJ_COL_TILE)
    n_col = d // tn
    w_specs = [pl.BlockSpec((d, tn), functools.partial(lambda i, j, c: (0, c * n_col + j), c=c))
               for c in range(5)]
    out_spec = pl.BlockSpec((tm, tn), lambda i, j: (i, j))
    out_shape = [jax.ShapeDtypeStruct((t, d), dt) for dt in (BF16, F32, F32, BF16, BF16, BF16, BF16)]
    return pl.pallas_call(
        _proj_kernel,
        out_shape=out_shape,
        grid=(t // tm, n_col),
        in_specs=[pl.BlockSpec((tm, d), lambda i, j: (i, 0)),
                  pl.BlockSpec((1, d), lambda i, j: (0, 0))] + w_specs,
        out_specs=[out_spec] * 7,
        scratch_shapes=[pltpu.VMEM((tm, d), BF16)],
        compiler_params=_params("parallel", "arbitrary"),
        name="project",
    )(h, g, w_in, w_in, w_in, w_in, w_in)


def _bias_of_distance(d, tab_ref, col):
    v = jnp.full(d.shape, tab_ref[0, col], F32)
    for b in range(1, N_BUCKETS):
        if b < N_BUCKETS - 1 and BUCKET_THR[b - 1] == BUCKET_THR[b]:
            continue
        v = jnp.where(d >= BUCKET_THR[b - 1], tab_ref[b, col], v)
    return v


def _prompt_bias_kernel(tab_ref, o_ref, *, tile):
    col = pl.program_id(0)
    i = lax.broadcasted_iota(jnp.int32, (tile, tile), 0)
    j = lax.broadcasted_iota(jnp.int32, (tile, tile), 1)
    o_ref[0] = _bias_of_distance(i - j, tab_ref, col)
    o_ref[1] = _bias_of_distance(i - j + tile, tab_ref, col)


def _prompt_bias_tiles(rel_bias, tile):
    n_col = rel_bias.shape[1]
    return pl.pallas_call(
        functools.partial(_prompt_bias_kernel, tile=tile),
        out_shape=jax.ShapeDtypeStruct((n_col, 2, tile, tile), F32),
        grid=(n_col,),
        in_specs=[pl.BlockSpec(memory_space=pltpu.SMEM)],
        out_specs=pl.BlockSpec((None, 2, tile, tile), lambda c: (c, 0, 0, 0)),
        compiler_params=_params("parallel"),
        name="prompt_bias_tiles",
    )(rel_bias)


def _sample_bias_kernel(tab_ref, o_ref, *, n_groups, col_of_group, dec, last_width):
    width = o_ref.shape[-1]
    i = lax.broadcasted_iota(jnp.int32, (dec, width), 0)
    j = lax.broadcasted_iota(jnp.int32, (dec, width), 1)
    for g in range(n_groups):
        col = col_of_group(g)
        rows = slice(g * dec, (g + 1) * dec)
        o_ref[0, rows, :] = _bias_of_distance(last_width + i - j, tab_ref, col)
        o_ref[1, rows, :] = jnp.full((dec, width), tab_ref[N_BUCKETS - 1, col], F32)
        new = _bias_of_distance(i - j, tab_ref, col)
        o_ref[2, rows, :] = jnp.where((j <= i) & (j < dec), new, NEG_INF)


def _sample_bias_tiles(rel_bias, n_groups, col_of_group, dec, last_width):
    width = max(last_width, LANES)
    return pl.pallas_call(
        functools.partial(_sample_bias_kernel, n_groups=n_groups, col_of_group=col_of_group, dec=dec,
                          last_width=last_width),
        out_shape=jax.ShapeDtypeStruct((3, n_groups * dec, width), F32),
        in_specs=[pl.BlockSpec(memory_space=pltpu.SMEM)],
        name="sample_bias_tiles",
    )(rel_bias)


def _lambda_value(lq1_ref, lk1_ref, lq2_ref, lk2_ref, lam_init):
    e1 = jnp.exp(jnp.sum(lq1_ref[...] * lk1_ref[...], axis=-1, keepdims=True))
    e2 = jnp.exp(jnp.sum(lq2_ref[...] * lk2_ref[...], axis=-1, keepdims=True))
    return e1 - e2 + lam_init


def _topk_block_mask(gate, k):
    n = gate.shape[-1]
    lane = lax.broadcasted_iota(jnp.int32, gate.shape, 1).astype(F32)
    mask = jnp.full(gate.shape, NEG_INF, F32)
    for _ in range(k):
        mx = jnp.max(gate, axis=-1, keepdims=True)
        first = jnp.min(jnp.where(gate == mx, lane, float(n)), axis=-1, keepdims=True)
        hit = lane == first
        mask = jnp.where(hit & (mx > NEG_INF), 0.0, mask)
        gate = jnp.where(hit, NEG_INF, gate)
    return mask


def _bf16_split_rows(mats, n_pad):
    parts = []
    for x in mats:
        rest = x
        for _ in range(3):
            piece = rest.astype(BF16).astype(F32)
            parts.append(_pad_rows(piece, n_pad))
            rest = rest - piece
    return jnp.concatenate(parts, axis=0).astype(BF16)


def _topk_rank_mask(gate, n, k):
    row_id = lax.broadcasted_iota(jnp.int32, gate.shape, 0)
    rank = jnp.zeros(gate.shape, F32)
    for other in range(n):
        cand = gate[other:other + 1, :]
        wins_tie = jnp.where(row_id > other, 1.0, 0.0)
        rank = rank + jnp.where(cand > gate, 1.0, jnp.where(cand == gate, wins_tie, 0.0))
    return jnp.where(rank < float(k), jnp.where(gate > NEG_INF, 0.0, NEG_INF), NEG_INF)


def _prompt_attn_kernel(*refs, mode, tile, n_q, n_heads_a, lam_init):
    if mode == "diff":
        (tab_ref, q_ref, k_ref, v_ref, b1_ref, b2_ref, lq1_ref, lk1_ref, lq2_ref, lk2_ref, gain_ref,
         o_ref, s_ref, p_ref) = refs
    else:
        (tab_ref, q_ref, k_ref, v_ref, b1_ref, b2_ref, kmean_ref, o_ref, s_ref, p_ref) = refs
    g = pl.program_id(1)
    half = LANES // 2
    lane = lax.broadcasted_iota(jnp.int32, (1, LANES), 1)
    bias_refs = (b1_ref, b2_ref)
    if mode == "diff":
        cols = (g, n_heads_a + g)
        lam = _lambda_value(lq1_ref, lk1_ref, lq2_ref, lk2_ref, lam_init)
    else:
        cols = (2 * n_heads_a + 2 * g, 2 * n_heads_a + 2 * g + 1)
        n_blk = kmean_ref.shape[0]
        n_pad = -(-n_blk // SUBLANES) * SUBLANES
        assert 2 * n_pad <= LANES
        kmean = kmean_ref[...]
        kmean_maps = (jnp.where(lane < half, kmean, 0.0), jnp.where(lane >= half, kmean, 0.0))
        gates = _dot_nt(_bf16_split_rows(kmean_maps, n_pad), q_ref[...])
        blk_id = lax.broadcasted_iota(jnp.int32, (n_pad, n_q * tile), 0)
        q_blk = lax.broadcasted_iota(jnp.int32, (n_pad, n_q * tile), 1) // tile
        sel_t = []
        for m in range(2):
            gate = gates[3 * m * n_pad:(3 * m + 1) * n_pad] + gates[(3 * m + 1) * n_pad:(3 * m + 2) * n_pad] \
                + gates[(3 * m + 2) * n_pad:(3 * m + 3) * n_pad]
            gate = jnp.where(blk_id < q_blk, gate, NEG_INF)
            sel_t.append(_topk_rank_mask(gate, n_blk, min(MOBA_TOPK, n_blk)))
    far = [tab_ref[N_BUCKETS - 1, cols[m]] for m in range(2)]
    row = lax.broadcasted_iota(jnp.int32, (tile, tile), 0)
    colv = lax.broadcasted_iota(jnp.int32, (tile, tile), 1)
    causal = row >= colv

    for qi in range(n_q):
        q = q_ref[qi * tile:(qi + 1) * tile, :]
        q_maps = (jnp.where(lane < half, q, jnp.zeros_like(q)), jnp.where(lane >= half, q, jnp.zeros_like(q)))
        if mode == "moba" and qi > 0:
            packed = [sel_t[m][:, qi * tile:(qi + 1) * tile] for m in range(2)]
            packed.append(jnp.zeros((LANES - 2 * n_pad, tile), F32))
            sel_q = jnp.concatenate(packed, axis=0).T
        outs = []
        for m in range(2):
            run_max = None
            for j in range(qi + 1):
                s = _dot_nt(q_maps[m], k_ref[j * tile:(j + 1) * tile, :])
                if j == qi:
                    s = jnp.where(causal, s + bias_refs[m][0], NEG_INF)
                else:
                    shift = bias_refs[m][1] if j == qi - 1 else far[m]
                    if mode == "moba":
                        shift = shift + sel_q[:, m * n_pad + j:m * n_pad + j + 1]
                    s = s + shift
                s_ref[m, j] = s
                run_max = s if run_max is None else jnp.maximum(run_max, s)
            mx = jnp.max(run_max, axis=-1, keepdims=True)
            run_sum = None
            for j in range(qi + 1):
                p = jnp.exp(s_ref[m, j] - mx)
                run_sum = p if run_sum is None else run_sum + p
                p_ref[m, :, j * tile:(j + 1) * tile] = p.astype(BF16)
            n_keys = (qi + 1) * tile
            denom = jnp.sum(run_sum, axis=-1, keepdims=True)
            outs.append(_dot(p_ref[m, :, :n_keys], v_ref[:n_keys, :]) / denom)
        rows = slice(qi * tile, (qi + 1) * tile)
        if mode == "diff":
            o = outs[0] - lam * outs[1]
            o_ref[rows, :] = (_rms(o) * gain_ref[...] * (1.0 - lam_init)).astype(o_ref.dtype)
        else:
            o_ref[rows, :] = jnp.where(lane < half, outs[0], outs[1]).astype(o_ref.dtype)


def _prompt_attention(mode, rel_bias, bias_tiles, q, kb, vb, batch, seq, width_a, extras, lam_init=0.0):
    tile = ATT_TILE
    n_q = seq // tile
    n_heads_a = width_a // (2 * D_A)
    if mode == "diff":
        n_groups = width_a // LANES
        col0 = 0
        cols = (lambda g: g, lambda g: n_heads_a + g)
    else:
        n_groups = (q.shape[1] - width_a) // LANES
        col0 = width_a // LANES
        cols = (lambda g: 2 * n_heads_a + 2 * g, lambda g: 2 * n_heads_a + 2 * g + 1)
    seq_spec = pl.BlockSpec((seq, LANES), lambda b, g: (b, col0 + g))
    in_specs = [
        pl.BlockSpec(memory_space=pltpu.SMEM),
        seq_spec, seq_spec, seq_spec,
        pl.BlockSpec((None, 2, tile, tile), lambda b, g: (cols[0](g), 0, 0, 0)),
        pl.BlockSpec((None, 2, tile, tile), lambda b, g: (cols[1](g), 0, 0, 0)),
    ]
    args = [rel_bias, q, kb, vb, bias_tiles, bias_tiles]
    if mode == "diff":
        lq1, lk1, lq2, lk2, gain = extras
        in_specs += [pl.BlockSpec((1, D_A), lambda b, g: (0, 0))] * 4
        in_specs += [pl.BlockSpec((1, LANES), lambda b, g: (0, 0))]
        args += [lq1, lk1, lq2, lk2, gain]
    else:
        (kmean,) = extras
        n_blk = kmean.shape[1]
        in_specs += [pl.BlockSpec((None, n_blk, LANES), lambda b, g: (b, 0, g))]
        args += [kmean]
    return pl.pallas_call(
        functools.partial(_prompt_attn_kernel, mode=mode, tile=tile, n_q=n_q, n_heads_a=n_heads_a,
                          lam_init=lam_init),
        out_shape=jax.ShapeDtypeStruct((batch * seq, n_groups * LANES), BF16),
        grid=(batch, n_groups),
        in_specs=in_specs,
        out_specs=pl.BlockSpec((seq, LANES), lambda b, g: (b, g)),
        scratch_shapes=[pltpu.VMEM((2, n_q, tile, tile), F32), pltpu.VMEM((2, tile, seq), BF16)],
        compiler_params=_params("parallel", "parallel"),
        name="prompt_" + mode,
    )(*args)


def _block_mean_kernel(k_ref, o_ref):
    o_ref[...] = jnp.mean(k_ref[...], axis=0, keepdims=True)


def _prompt_block_means(k, width_a, width_b):
    assert width_a % width_b == 0
    n_blk = k.shape[0] // MOBA_BLOCK
    return pl.pallas_call(
        _block_mean_kernel,
        out_shape=jax.ShapeDtypeStruct((n_blk, 1, width_b), F32),
        grid=(n_blk,),
        in_specs=[pl.BlockSpec((MOBA_BLOCK, width_b), lambda i: (i, width_a // width_b))],
        out_specs=pl.BlockSpec((None, 1, width_b), lambda i: (i, 0, 0)),
        compiler_params=_params("parallel"),
        name="prompt_block_means",
    )(k)


def _block_diag_queries(q, n_groups, group_width):
    lane = lax.broadcasted_iota(jnp.int32, q.shape, 1)
    rows = [jnp.where((lane >= g * group_width) & (lane < (g + 1) * group_width), q, jnp.zeros_like(q))
            for g in range(n_groups)]
    return jnp.concatenate(rows, axis=0)


def _gather_block_diag(x, n_groups, rows_per_group, group_width):
    lane = lax.broadcasted_iota(jnp.int32, (rows_per_group, x.shape[1]), 1)
    out = jnp.zeros((rows_per_group, x.shape[1]), x.dtype)
    for g in range(n_groups):
        part = x[g * rows_per_group:(g + 1) * rows_per_group, :]
        out = jnp.where((lane >= g * group_width) & (lane < (g + 1) * group_width), part, out)
    return out


def _pad_rows(x, n_rows):
    if x.shape[0] == n_rows:
        return x
    return jnp.concatenate([x, jnp.zeros((n_rows - x.shape[0], x.shape[1]), x.dtype)], axis=0)


def _softmax_start(s, v, m_ref, l_ref, acc_ref):
    mx = jnp.max(s, axis=-1, keepdims=True)
    p = jnp.exp(s - mx)
    m_ref[...] = mx
    l_ref[...] = jnp.sum(p, axis=-1, keepdims=True)
    acc_ref[...] = _dot(p.astype(BF16), v)


def _softmax_step(s, v, m_ref, l_ref, acc_ref):
    m_old = m_ref[...]
    mx = jnp.maximum(m_old, jnp.max(s, axis=-1, keepdims=True))
    alpha = jnp.exp(m_old - mx)
    p = jnp.exp(s - mx)
    l_ref[...] = alpha * l_ref[...] + jnp.sum(p, axis=-1, keepdims=True)
    acc_ref[...] = alpha * acc_ref[...] + _dot(p.astype(BF16), v)
    m_ref[...] = mx


def _sample_diff_kernel(pt_ref, *refs, n_pages_step, page, dec, n_heads, lam_init):
    k_refs = refs[:n_pages_step]
    v_refs = refs[n_pages_step:2 * n_pages_step]
    (q_ref, kn_ref, vn_ref, bias_ref, lq1_ref, lk1_ref, lq2_ref, lk2_ref, gain_ref,
     o_ref, qb_ref, m_ref, l_ref, acc_ref) = refs[2 * n_pages_step:]
    c = pl.program_id(1)
    n_steps = pl.num_programs(1)

    @pl.when(c == 0)
    def _():
        qb_ref[...] = _block_diag_queries(q_ref[...].astype(F32), 2 * n_heads, D_A).astype(BF16)
        k_new = _pad_rows(kn_ref[...], LANES).astype(BF16)
        v_new = _pad_rows(vn_ref[...], LANES).astype(BF16)
        s = _dot_nt(qb_ref[...], k_new) + bias_ref[2]
        _softmax_start(s, v_new, m_ref, l_ref, acc_ref)

    qb = qb_ref[...]
    scores = []
    for p in range(n_pages_step):
        s = _dot_nt(qb, k_refs[p][...].astype(BF16))
        if p == n_pages_step - 1:
            s = s + jnp.where(c == n_steps - 1, bias_ref[0], bias_ref[1])
        else:
            s = s + bias_ref[1]
        scores.append(s)
    s = jnp.concatenate(scores, axis=1)
    v = jnp.concatenate([v_refs[p][...].astype(BF16) for p in range(n_pages_step)], axis=0)
    _softmax_step(s, v, m_ref, l_ref, acc_ref)

    @pl.when(c == n_steps - 1)
    def _():
        on = acc_ref[...] / l_ref[...]
        lane = lax.broadcasted_iota(jnp.int32, (dec, on.shape[1]), 1)
        o1 = jnp.zeros((dec, on.shape[1]), F32)
        o2 = jnp.zeros((dec, on.shape[1]), F32)
        for h in range(n_heads):
            in_head = (lane >= h * 2 * D_A) & (lane < (h + 1) * 2 * D_A)
            o1 = jnp.where(in_head, on[(2 * h) * dec:(2 * h + 1) * dec, :], o1)
            o2 = jnp.where(in_head, on[(2 * h + 1) * dec:(2 * h + 2) * dec, :], o2)
        lam = _lambda_value(lq1_ref, lk1_ref, lq2_ref, lk2_ref, lam_init)
        o = o1 - lam * o2
        for h in range(n_heads):
            cols = slice(h * 2 * D_A, (h + 1) * 2 * D_A)
            o_ref[:, cols] = (_rms(o[:, cols]) * gain_ref[...] * (1.0 - lam_init)).astype(o_ref.dtype)


def _sample_diff(page_table, cache_k, cache_v, layer, q, k_new, v_new, bias, lam_vecs, gain, width_a, lam_init):
    n_seq, n_pages = page_table.shape
    page = cache_k.shape[2]
    dec = q.shape[1]
    n_heads = width_a // (2 * D_A)
    pps = SAMPLE_DIFF_PAGES if n_pages % SAMPLE_DIFF_PAGES == 0 else 1
    n_steps = n_pages // pps
    n_rows = 2 * n_heads * dec
    assert page == LANES and 2 * D_A == LANES and bias.shape == (3, n_rows, LANES)

    def page_spec(p):
        return pl.BlockSpec((None, None, page, width_a),
                            lambda b, c, pt: (pt[b * n_pages + c * pps + p], layer, 0, 0))

    row_spec = pl.BlockSpec((None, dec, width_a), lambda b, c, pt: (b, 0, 0))
    vec_spec = pl.BlockSpec((1, D_A), lambda b, c, pt: (0, 0))
    in_specs = ([page_spec(p) for p in range(pps)] * 2
                + [row_spec, row_spec, row_spec,
                   pl.BlockSpec((3, n_rows, LANES), lambda b, c, pt: (0, 0, 0)),
                   vec_spec, vec_spec, vec_spec, vec_spec,
                   pl.BlockSpec((1, LANES), lambda b, c, pt: (0, 0))])
    return pl.pallas_call(
        functools.partial(_sample_diff_kernel, n_pages_step=pps, page=page, dec=dec, n_heads=n_heads,
                          lam_init=lam_init),
        out_shape=jax.ShapeDtypeStruct((n_seq, dec, width_a), BF16),
        grid_spec=pltpu.PrefetchScalarGridSpec(
            num_scalar_prefetch=1,
            grid=(n_seq, n_steps),
            in_specs=in_specs,
            out_specs=pl.BlockSpec((None, dec, width_a), lambda b, c, pt: (b, 0, 0)),
            scratch_shapes=[pltpu.VMEM((n_rows, width_a), BF16), pltpu.VMEM((n_rows, 1), F32),
                            pltpu.VMEM((n_rows, 1), F32), pltpu.VMEM((n_rows, width_a), F32)]),
        compiler_params=_params("parallel", "arbitrary"),
        name="sample_diff",
    )(page_table.reshape(-1), *([cache_k] * pps), *([cache_v] * pps), q, k_new, v_new, bias, *lam_vecs, gain)


def _sample_moba_kernel(pt_ref, *refs, n_pages_step, n_steps, dec, n_heads, n_blk, pages_per_block):
    k_refs = refs[:n_pages_step]
    v_refs = refs[n_pages_step:2 * n_pages_step]
    (q_ref, kn_ref, vn_ref, bias_ref,
     o_ref, qb_ref, s_ref, smax_ref, kmean_ref, sel_ref, m_ref, l_ref, acc_ref) = refs[2 * n_pages_step:]
    c = pl.program_id(1)
    page = k_refs[0].shape[0]
    blocks_step = n_pages_step // pages_per_block
    blk_lane = lax.broadcasted_iota(jnp.int32, smax_ref.shape, 1)

    @pl.when(c == 0)
    def _():
        qb_ref[...] = _block_diag_queries(q_ref[...].astype(F32), n_heads, D_B).astype(BF16)
        smax_ref[...] = jnp.full(smax_ref.shape, NEG_INF, F32)
        kmean_ref[...] = jnp.zeros_like(kmean_ref)

    @pl.when(c < n_steps)
    def _():
        qb = qb_ref[...]
        far = bias_ref[1][:, :page]
        smax = smax_ref[...]
        for blk in range(blocks_step):
            blk_max = None
            for p in range(blk * pages_per_block, (blk + 1) * pages_per_block):
                s = _dot_nt(qb, k_refs[p][...].astype(BF16))
                in_last_block = p - (n_pages_step - pages_per_block)
                if in_last_block >= 0:
                    near = bias_ref[0][:, in_last_block * page:(in_last_block + 1) * page]
                    s = s + jnp.where(c == n_steps - 1, near, far)
                else:
                    s = s + far
                s_ref[c, :, p * page:(p + 1) * page] = s
                blk_max = s if blk_max is None else jnp.maximum(blk_max, s)
            smax = jnp.where(blk_lane == c * blocks_step + blk, jnp.max(blk_max, axis=-1, keepdims=True), smax)
        smax_ref[...] = smax
        means = []
        for blk in range(blocks_step):
            total = k_refs[blk * pages_per_block][...]
            for p in range(1, pages_per_block):
                total = total + k_refs[blk * pages_per_block + p][...]
            means.append(jnp.sum(total, axis=0, keepdims=True) * (1.0 / MOBA_BLOCK))
        first_blk = pl.multiple_of(c * blocks_step, blocks_step)
        kmean_ref[pl.ds(first_blk, blocks_step), :] = jnp.concatenate(means, axis=0)

    @pl.when(c == n_steps)
    def _():
        qb = qb_ref[...]
        parts = _dot_nt(qb, _bf16_split_rows([kmean_ref[...]], LANES))
        gate = parts[:, :LANES] + parts[:, LANES:2 * LANES] + parts[:, 2 * LANES:]
        gate = jnp.where(blk_lane < n_blk, gate, NEG_INF)
        sel = _topk_block_mask(gate, min(MOBA_TOPK, n_blk))
        sel_ref[...] = sel
        k_new = _pad_rows(kn_ref[...], LANES).astype(BF16)
        v_new = _pad_rows(vn_ref[...], LANES).astype(BF16)
        s_new = _dot_nt(qb, k_new) + bias_ref[2][:, :LANES]
        mx = jnp.maximum(jnp.max(s_new, axis=-1, keepdims=True),
                         jnp.max(smax_ref[...] + sel, axis=-1, keepdims=True))
        p = jnp.exp(s_new - mx)
        m_ref[...] = mx
        l_ref[...] = jnp.sum(p, axis=-1, keepdims=True)
        acc_ref[...] = _dot(p.astype(BF16), v_new)

    @pl.when(c >= n_steps)
    def _():
        step = c - n_steps
        sel = sel_ref[...]
        mx = m_ref[...]
        parts = []
        run_sum = None
        for blk in range(blocks_step):
            picked = jnp.max(jnp.where(blk_lane == step * blocks_step + blk, sel, NEG_INF), axis=-1, keepdims=True)
            p = jnp.exp(s_ref[step, :, blk * MOBA_BLOCK:(blk + 1) * MOBA_BLOCK] + (picked - mx))
            run_sum = p if run_sum is None else run_sum + p
            parts.append(p.astype(BF16))
        p = jnp.concatenate(parts, axis=1)
        v = jnp.concatenate([v_refs[i][...].astype(BF16) for i in range(n_pages_step)], axis=0)
        l_ref[...] += jnp.sum(run_sum, axis=-1, keepdims=True)
        acc_ref[...] += _dot(p, v)

    @pl.when(c == 2 * n_steps - 1)
    def _():
        on = acc_ref[...] / l_ref[...]
        o_ref[...] = _gather_block_diag(on, n_heads, dec, D_B).astype(o_ref.dtype)


def _sample_moba(page_table, cache_k, cache_v, layer, q, k_new, v_new, bias, width_a, width_b):
    n_seq, n_pages = page_table.shape
    page = cache_k.shape[2]
    dec = q.shape[1]
    n_heads = width_b // D_B
    pps = SAMPLE_MOBA_PAGES
    ppb = MOBA_BLOCK // page
    n_blk = n_pages // ppb
    assert n_pages % pps == 0 and pps // ppb == SUBLANES and width_a % width_b == 0 and n_blk <= LANES
    n_steps = n_pages // pps
    n_rows = n_heads * dec
    col_blk = width_a // width_b
    assert bias.shape == (3, n_rows, MOBA_BLOCK)

    def key_spec(p):
        return pl.BlockSpec((None, None, page, width_b), lambda b, c, pt: (
            pt[b * n_pages + jnp.minimum(c, n_steps - 1) * pps + p], layer, 0, col_blk))

    def value_spec(p):
        return pl.BlockSpec((None, None, page, width_b), lambda b, c, pt: (
            pt[b * n_pages + jnp.maximum(c - n_steps, 0) * pps + p], layer, 0, col_blk))

    row_spec = pl.BlockSpec((None, dec, width_b), lambda b, c, pt: (b, 0, col_blk))
    return pl.pallas_call(
        functools.partial(_sample_moba_kernel, n_pages_step=pps, n_steps=n_steps, dec=dec, n_heads=n_heads,
                          n_blk=n_blk, pages_per_block=ppb),
        out_shape=jax.ShapeDtypeStruct((n_seq, dec, width_b), BF16),
        grid_spec=pltpu.PrefetchScalarGridSpec(
            num_scalar_prefetch=1,
            grid=(n_seq, 2 * n_steps),
            in_specs=[key_spec(p) for p in range(pps)] + [value_spec(p) for p in range(pps)]
            + [row_spec, row_spec, row_spec,
               pl.BlockSpec((3, n_rows, MOBA_BLOCK), lambda b, c, pt: (0, 0, 0))],
            out_specs=pl.BlockSpec((None, dec, width_b), lambda b, c, pt: (b, 0, 0)),
            scratch_shapes=[pltpu.VMEM((n_rows, width_b), BF16),
                            pltpu.VMEM((n_steps, n_rows, pps * page), F32),
                            pltpu.VMEM((n_rows, LANES), F32),
                            pltpu.VMEM((LANES, width_b), F32),
                            pltpu.VMEM((n_rows, LANES), F32),
                            pltpu.VMEM((n_rows, 1), F32), pltpu.VMEM((n_rows, 1), F32),
                            pltpu.VMEM((n_rows, width_b), F32)]),
        compiler_params=_params("parallel", "arbitrary"),
        name="sample_moba",
    )(page_table.reshape(-1), *([cache_k] * pps), *([cache_v] * pps), q, k_new, v_new, bias)


def _merge_kernel(oa_ref, ob_ref, sa_ref, sb_ref, h_ref, wa_ref, wb_ref, wo_ref, o_ref):
    m = (sa_ref[...].astype(F32) * _dot(oa_ref[...], wa_ref[...])
         + sb_ref[...].astype(F32) * _dot(ob_ref[...], wb_ref[...]))
    o_ref[...] = h_ref[...] + _dot(m.astype(BF16), wo_ref[...])


def _merge(oa, ob, sa, sb, h, w_up_a, w_up_b, w_out):
    t, d = h.shape
    tm = _row_tile(t, 256)
    wa, wb = oa.shape[1], ob.shape[1]
    row = lambda w: pl.BlockSpec((tm, w), lambda i: (i, 0))
    full = lambda a: pl.BlockSpec(a.shape, lambda i: (0, 0))
    return pl.pallas_call(
        _merge_kernel,
        out_shape=jax.ShapeDtypeStruct((t, d), F32),
        grid=(t // tm,),
        in_specs=[row(wa), row(wb), row(d), row(d), row(d), full(w_up_a), full(w_up_b), full(w_out)],
        out_specs=row(d),
        compiler_params=_params("parallel"),
        name="merge",
    )(oa, ob, sa, sb, h, w_up_a, w_up_b, w_out)


def _pad_cols(w, mult):
    pad = (-w.shape[1]) % mult
    return jnp.pad(w, ((0, 0), (0, pad)))


def _pad_rows_to(w, mult):
    pad = (-w.shape[0]) % mult
    return jnp.pad(w, ((0, pad), (0, 0)))


def kernel(x_prompt, x_sample, cache_k, cache_v, page_table, rel_bias, ffn1_norm, ffn1_w_gate, ffn1_w_up,
           ffn1_w_down, mix_norm, w_in, lambda_q1, lambda_k1, lambda_q2, lambda_k2, subln_gain, w_up_a,
           w_up_b, w_out, ffn2_norm, ffn2_w_gate, ffn2_w_up, ffn2_w_down, final_norm):
    batch, seq, d_model = x_prompt.shape
    n_seq, dec, _ = x_sample.shape
    depth = w_in.shape[0]
    width_a = w_up_a.shape[1]
    width_b = w_up_b.shape[1]
    kv_width = width_a + width_b
    n_heads_a = width_a // (2 * D_A)
    n_heads_b = width_b // D_B
    page = cache_k.shape[2]
    assert seq % ATT_TILE == 0 and ATT_TILE >= FAR_DISTANCE and page >= FAR_DISTANCE
    assert rel_bias.shape == (N_BUCKETS, 2 * n_heads_a + n_heads_b)

    hp = x_prompt.reshape(batch * seq, d_model)
    hs = x_sample.reshape(n_seq * dec, d_model)
    row = lambda a: a.reshape(1, -1)

    prompt_tiles = _prompt_bias_tiles(rel_bias, ATT_TILE)
    diff_tiles = _sample_bias_tiles(rel_bias, 2 * n_heads_a, lambda g: (g % 2) * n_heads_a + g // 2, dec, page)
    moba_tiles = _sample_bias_tiles(rel_bias, n_heads_b, lambda g: 2 * n_heads_a + g, dec, MOBA_BLOCK)

    k_p, v_p, k_s, v_s = [], [], [], []
    for l in range(depth):
        lam_init = 0.8 - 0.6 * math.exp(-0.3 * l)
        lam_vecs = [row(a[l]) for a in (lambda_q1, lambda_k1, lambda_q2, lambda_k2)]
        gain = row(subln_gain[l])

        def ffn_weights(wg, wu, wd):
            return (_pad_cols(wg[l], FFN_COL_TILE).astype(BF16), _pad_cols(wu[l], FFN_COL_TILE).astype(BF16),
                    _pad_rows_to(wd[l], FFN_COL_TILE).astype(BF16))

        w1 = ffn_weights(ffn1_w_gate, ffn1_w_up, ffn1_w_down)
        hp = _ffn(hp, row(ffn1_norm[l]), *w1)
        hs = _ffn(hs, row(ffn1_norm[l]), *w1)

        w_in_b = w_in[l].astype(BF16)
        qp, kp, vp, kpb, vpb, sap, sbp = _project(hp, row(mix_norm[l]), w_in_b, kv_width)
        qs, ks, vs, _, _, sas, sbs = _project(hs, row(mix_norm[l]), w_in_b, kv_width)

        oa_p = _prompt_attention("diff", rel_bias, prompt_tiles, qp, kpb, vpb, batch, seq, width_a,
                                 lam_vecs + [gain], lam_init)
        kmean_p = _prompt_block_means(kp, width_a, width_b).reshape(batch, seq // MOBA_BLOCK, width_b)
        ob_p = _prompt_attention("moba", rel_bias, prompt_tiles, qp, kpb, vpb, batch, seq, width_a, [kmean_p])

        qs3 = qs.reshape(n_seq, dec, kv_width)
        ks3 = ks.reshape(n_seq, dec, kv_width)
        vs3 = vs.reshape(n_seq, dec, kv_width)
        oa_s = _sample_diff(page_table, cache_k, cache_v, l, qs3, ks3, vs3, diff_tiles, lam_vecs, gain,
                            width_a, lam_init)
        ob_s = _sample_moba(page_table, cache_k, cache_v, l, qs3, ks3, vs3, moba_tiles, width_a, width_b)

        wa_b, wb_b, wo_b = w_up_a[l].astype(BF16), w_up_b[l].astype(BF16), w_out[l].astype(BF16)
        hp = _merge(oa_p, ob_p, sap, sbp, hp, wa_b, wb_b, wo_b)
        hs = _merge(oa_s.reshape(n_seq * dec, width_a), ob_s.reshape(n_seq * dec, width_b), sas, sbs, hs,
                    wa_b, wb_b, wo_b)

        w2 = ffn_weights(ffn2_w_gate, ffn2_w_up, ffn2_w_down)
        last = l == depth - 1
        fg = row(final_norm) if last else None
        hp = _ffn(hp, row(ffn2_norm[l]), *w2, final_g=fg)
        hs = _ffn(hs, row(ffn2_norm[l]), *w2, final_g=fg)

        k_p.append(kp.reshape(batch, seq, kv_width))
        v_p.append(vp.reshape(batch, seq, kv_width))
        k_s.append(ks3)
        v_s.append(vs3)

    return (hp.reshape(batch, seq, d_model), hs.reshape(n_seq, dec, d_model),
            jnp.stack(k_p, axis=1), jnp.stack(v_p, axis=1), jnp.stack(k_s, axis=1), jnp.stack(v_s, axis=1))
```

```python
import functools
import math

import numpy as np
import jax
import jax.numpy as jnp
from jax import lax
from jax.experimental import pallas as pl
from jax.experimental.pallas import tpu as pltpu

F32 = jnp.float32
BF16 = jnp.bfloat16
NEG_INF = float("-inf")

D_A = 64
D_B = 64
MOBA_BLOCK = 256
MOBA_TOPK = 3
N_BUCKETS = 32
MAX_DISTANCE = 128
EPS = 1e-6
QK_SCALE = D_A ** -0.5

LANES = 128
SUBLANES = 8
VMEM_LIMIT_BYTES = 56 * 1024 * 1024

ATT_TILE = MOBA_BLOCK
FFN_COL_TILE = 512
PROJ_COL_TILE = 512
SAMPLE_DIFF_PAGES = 16
SAMPLE_MOBA_PAGES = 16


def _bucket_thresholds():
    n = np.arange(0, 4 * MAX_DISTANCE, dtype=np.int32)
    max_exact = N_BUCKETS // 2
    nf = np.maximum(n, max_exact).astype(np.float32)
    log_b = max_exact + (np.log(nf / np.float32(max_exact)) / np.float32(math.log(MAX_DISTANCE / max_exact))
                         * np.float32(N_BUCKETS - max_exact)).astype(np.int32)
    bucket = np.where(n < max_exact, n, np.minimum(log_b, N_BUCKETS - 1))
    assert bucket[-1] == N_BUCKETS - 1
    return [int(np.argmax(bucket >= b)) for b in range(1, N_BUCKETS)]


BUCKET_THR = _bucket_thresholds()
FAR_DISTANCE = BUCKET_THR[-1]


def _params(*sem):
    return pltpu.CompilerParams(dimension_semantics=sem, vmem_limit_bytes=VMEM_LIMIT_BYTES)


def _rms(x):
    return x * lax.rsqrt(jnp.mean(x * x, axis=-1, keepdims=True) + EPS)


def _dot(a, b):
    return jnp.dot(a, b, preferred_element_type=F32)


def _dot_nt(a, b, precision=None):
    return lax.dot_general(a, b, (((1,), (1,)), ((), ())), preferred_element_type=F32, precision=precision)


def _row_tile(n, pref):
    t = min(n, pref)
    assert n % t == 0
    return t


def _ffn_kernel(x_ref, g_ref, wg_ref, wu_ref, wd_ref, *rest, n_col, final):
    if final:
        fg_ref, o_ref, hn_ref = rest
    else:
        o_ref, hn_ref = rest
    j = pl.program_id(1)

    @pl.when(j == 0)
    def _():
        hn_ref[...] = (_rms(x_ref[...]) * g_ref[...]).astype(BF16)
        o_ref[...] = jnp.zeros_like(o_ref)

    hn = hn_ref[...]
    a = _dot(hn, wg_ref[...])
    b = _dot(hn, wu_ref[...])
    act = (a * jax.nn.sigmoid(a) * b).astype(BF16)
    o_ref[...] += _dot(act, wd_ref[...])

    @pl.when(j == n_col - 1)
    def _():
        h = x_ref[...] + 0.5 * o_ref[...]
        if final:
            h = _rms(h) * fg_ref[...]
        o_ref[...] = h


def _ffn(x, g, wg, wu, wd, final_g=None):
    t, d = x.shape
    f = wg.shape[1]
    tm = _row_tile(t, 512)
    tf = FFN_COL_TILE
    n_col = f // tf
    final = final_g is not None
    in_specs = [
        pl.BlockSpec((tm, d), lambda i, j: (i, 0)),
        pl.BlockSpec((1, d), lambda i, j: (0, 0)),
        pl.BlockSpec((d, tf), lambda i, j: (0, j)),
        pl.BlockSpec((d, tf), lambda i, j: (0, j)),
        pl.BlockSpec((tf, d), lambda i, j: (j, 0)),
    ]
    args = [x, g, wg, wu, wd]
    if final:
        in_specs.append(pl.BlockSpec((1, d), lambda i, j: (0, 0)))
        args.append(final_g)
    return pl.pallas_call(
        functools.partial(_ffn_kernel, n_col=n_col, final=final),
        out_shape=jax.ShapeDtypeStruct((t, d), F32),
        grid=(t // tm, n_col),
        in_specs=in_specs,
        out_specs=pl.BlockSpec((tm, d), lambda i, j: (i, 0)),
        scratch_shapes=[pltpu.VMEM((tm, d), BF16)],
        compiler_params=_params("parallel", "arbitrary"),
        name="ffn_final" if final else "ffn",
    )(*args)


def _proj_kernel(h_ref, g_ref, wq_ref, wk_ref, wv_ref, wa_ref, wb_ref,
                 q_ref, k_ref, v_ref, kb_ref, vb_ref, sa_ref, sb_ref, hn_ref):
    @pl.when(pl.program_id(1) == 0)
    def _():
        hn_ref[...] = (_rms(h_ref[...]) * g_ref[...]).astype(BF16)

    hn = hn_ref[...]
    q_ref[...] = (_dot(hn, wq_ref[...]) * QK_SCALE).astype(BF16)
    k = _dot(hn, wk_ref[...])
    k_ref[...] = k
    kb_ref[...] = k.astype(BF16)
    v = _dot(hn, wv_ref[...])
    v_ref[...] = v
    vb_ref[...] = v.astype(BF16)
    sa_ref[...] = jax.nn.sigmoid(_dot(hn, wa_ref[...])).astype(BF16)
    sb_ref[...] = jax.nn.sigmoid(_dot(hn, wb_ref[...])).astype(BF16)


def _project(h, g, w_in, kv_width):
    t, d = h.shape
    assert kv_width == d and w_in.shape[1] == 5 * d
    tm = _row_tile(t, 512)
    tn = _row_tile(d, PROJ_COL_TILE)
    n_col = d // tn
    w_specs = [pl.BlockSpec((d, tn), functools.partial(lambda i, j, c: (0, c * n_col + j), c=c))
               for c in range(5)]
    out_spec = pl.BlockSpec((tm, tn), lambda i, j: (i, j))
    out_shape = [jax.ShapeDtypeStruct((t, d), dt) for dt in (BF16, F32, F32, BF16, BF16, BF16, BF16)]
    return pl.pallas_call(
        _proj_kernel,
        out_shape=out_shape,
        grid=(t // tm, n_col),
        in_specs=[pl.BlockSpec((tm, d), lambda i, j: (i, 0)),
                  pl.BlockSpec((1, d), lambda i, j: (0, 0))] + w_specs,
        out_specs=[out_spec] * 7,
        scratch_shapes=[pltpu.VMEM((tm, d), BF16)],
        compiler_params=_params("parallel", "arbitrary"),
        name="project",
    )(h, g, w_in, w_in, w_in, w_in, w_in)


def _bias_of_distance(d, tab_ref, col):
    v = jnp.full(d.shape, tab_ref[0, col], F32)
    for b in range(1, N_BUCKETS):
        if b < N_BUCKETS - 1 and BUCKET_THR[b - 1] == BUCKET_THR[b]:
            continue
        v = jnp.where(d >= BUCKET_THR[b - 1], tab_ref[b, col], v)
    return v


def _prompt_bias_kernel(tab_ref, o_ref, *, tile):
    col = pl.program_id(0)
    i = lax.broadcasted_iota(jnp.int32, (tile, tile), 0)
    j = lax.broadcasted_iota(jnp.int32, (tile, tile), 1)
    o_ref[0] = _bias_of_distance(i - j, tab_ref, col)
    o_ref[1] = _bias_of_distance(i - j + tile, tab_ref, col)


def _prompt_bias_tiles(rel_bias, tile):
    n_col = rel_bias.shape[1]
    return pl.pallas_call(
        functools.partial(_prompt_bias_kernel, tile=tile),
        out_shape=jax.ShapeDtypeStruct((n_col, 2, tile, tile), F32),
        grid=(n_col,),
        in_specs=[pl.BlockSpec(memory_space=pltpu.SMEM)],
        out_specs=pl.BlockSpec((None, 2, tile, tile), lambda c: (c, 0, 0, 0)),
        compiler_params=_params("parallel"),
        name="prompt_bias_tiles",
    )(rel_bias)


def _sample_bias_kernel(tab_ref, o_ref, *, n_groups, col_of_group, dec, last_width):
    width = o_ref.shape[-1]
    i = lax.broadcasted_iota(jnp.int32, (dec, width), 0)
    j = lax.broadcasted_iota(jnp.int32, (dec, width), 1)
    for g in range(n_groups):
        col = col_of_group(g)
        rows = slice(g * dec, (g + 1) * dec)
        o_ref[0, rows, :] = _bias_of_distance(last_width + i - j, tab_ref, col)
        o_ref[1, rows, :] = jnp.full((dec, width), tab_ref[N_BUCKETS - 1, col], F32)
        new = _bias_of_distance(i - j, tab_ref, col)
        o_ref[2, rows, :] = jnp.where((j <= i) & (j < dec), new, NEG_INF)


def _sample_bias_tiles(rel_bias, n_groups, col_of_group, dec, last_width):
    width = max(last_width, LANES)
    return pl.pallas_call(
        functools.partial(_sample_bias_kernel, n_groups=n_groups, col_of_group=col_of_group, dec=dec,
                          last_width=last_width),
        out_shape=jax.ShapeDtypeStruct((3, n_groups * dec, width), F32),
        in_specs=[pl.BlockSpec(memory_space=pltpu.SMEM)],
        name="sample_bias_tiles",
    )(rel_bias)


def _lambda_value(lq1_ref, lk1_ref, lq2_ref, lk2_ref, lam_init):
    e1 = jnp.exp(jnp.sum(lq1_ref[...] * lk1_ref[...], axis=-1, keepdims=True))
    e2 = jnp.exp(jnp.sum(lq2_ref[...] * lk2_ref[...], axis=-1, keepdims=True))
    return e1 - e2 + lam_init


def _topk_block_mask(gate, k):
    n = gate.shape[-1]
    lane = lax.broadcasted_iota(jnp.int32, gate.shape, 1).astype(F32)
    mask = jnp.full(gate.shape, NEG_INF, F32)
    for _ in range(k):
        mx = jnp.max(gate, axis=-1, keepdims=True)
        first = jnp.min(jnp.where(gate == mx, lane, float(n)), axis=-1, keepdims=True)
        hit = lane == first
        mask = jnp.where(hit & (mx > NEG_INF), 0.0, mask)
        gate = jnp.where(hit, NEG_INF, gate)
    return mask


def _bf16_split_rows(mats, n_pad):
    parts = []
    for x in mats:
        rest = x
        for _ in range(3):
            piece = rest.astype(BF16).astype(F32)
            parts.append(_pad_rows(piece, n_pad))
            rest = rest - piece
    return jnp.concatenate(parts, axis=0).astype(BF16)


def _topk_rank_mask(gate, n, k):
    row_id = lax.broadcasted_iota(jnp.int32, gate.shape, 0)
    rank = jnp.zeros(gate.shape, F32)
    for other in range(n):
        cand = gate[other:other + 1, :]
        wins_tie = jnp.where(row_id > other, 1.0, 0.0)
        rank = rank + jnp.where(cand > gate, 1.0, jnp.where(cand == gate, wins_tie, 0.0))
    return jnp.where(rank < float(k), jnp.where(gate > NEG_INF, 0.0, NEG_INF), NEG_INF)


def _prompt_attn_kernel(*refs, mode, tile, n_q, n_heads_a, lam_init):
    if mode == "diff":
        (tab_ref, q_ref, k_ref, v_ref, b1_ref, b2_ref, lq1_ref, lk1_ref, lq2_ref, lk2_ref, gain_ref,
         o_ref, s_ref, p_ref) = refs
    else:
        (tab_ref, q_ref, k_ref, v_ref, b1_ref, b2_ref, kmean_ref, o_ref, s_ref, p_ref) = refs
    g = pl.program_id(1)
    half = LANES // 2
    lane = lax.broadcasted_iota(jnp.int32, (1, LANES), 1)
    bias_refs = (b1_ref, b2_ref)
    if mode == "diff":
        cols = (g, n_heads_a + g)
        lam = _lambda_value(lq1_ref, lk1_ref, lq2_ref, lk2_ref, lam_init)
    else:
        cols = (2 * n_heads_a + 2 * g, 2 * n_heads_a + 2 * g + 1)
        n_blk = kmean_ref.shape[0]
        n_pad = -(-n_blk // SUBLANES) * SUBLANES
        assert 2 * n_pad <= LANES
        kmean = kmean_ref[...]
        kmean_maps = (jnp.where(lane < half, kmean, 0.0), jnp.where(lane >= half, kmean, 0.0))
        gates = _dot_nt(_bf16_split_rows(kmean_maps, n_pad), q_ref[...])
        blk_id = lax.broadcasted_iota(jnp.int32, (n_pad, n_q * tile), 0)
        q_blk = lax.broadcasted_iota(jnp.int32, (n_pad, n_q * tile), 1) // tile
        sel_t = []
        for m in range(2):
            gate = gates[3 * m * n_pad:(3 * m + 1) * n_pad] + gates[(3 * m + 1) * n_pad:(3 * m + 2) * n_pad] \
                + gates[(3 * m + 2) * n_pad:(3 * m + 3) * n_pad]
            gate = jnp.where(blk_id < q_blk, gate, NEG_INF)
            sel_t.append(_topk_rank_mask(gate, n_blk, min(MOBA_TOPK, n_blk)))
    far = [tab_ref[N_BUCKETS - 1, cols[m]] for m in range(2)]
    row = lax.broadcasted_iota(jnp.int32, (tile, tile), 0)
    colv = lax.broadcasted_iota(jnp.int32, (tile, tile), 1)
    causal = row >= colv

    for qi in range(n_q):
        q = q_ref[qi * tile:(qi + 1) * tile, :]
        q_maps = (jnp.where(lane < half, q, jnp.zeros_like(q)), jnp.where(lane >= half, q, jnp.zeros_like(q)))
        if mode == "moba" and qi > 0:
            packed = [sel_t[m][:, qi * tile:(qi + 1) * tile] for m in range(2)]
            packed.append(jnp.zeros((LANES - 2 * n_pad, tile), F32))
            sel_q = jnp.concatenate(packed, axis=0).T
        outs = []
        for m in range(2):
            run_max = None
            for j in range(qi + 1):
                s = _dot_nt(q_maps[m], k_ref[j * tile:(j + 1) * tile, :])
                if j == qi:
                    s = jnp.where(causal, s + bias_refs[m][0], NEG_INF)
                else:
                    shift = bias_refs[m][1] if j == qi - 1 else far[m]
                    if mode == "moba":
                        shift = shift + sel_q[:, m * n_pad + j:m * n_pad + j + 1]
                    s = s + shift
                s_ref[m, j] = s
                run_max = s if run_max is None else jnp.maximum(run_max, s)
            mx = jnp.max(run_max, axis=-1, keepdims=True)
            run_sum = None
            for j in range(qi + 1):
                p = jnp.exp(s_ref[m, j] - mx)
                run_sum = p if run_sum is None else run_sum + p
                p_ref[m, :, j * tile:(j + 1) * tile] = p.astype(BF16)
            n_keys = (qi + 1) * tile
            denom = jnp.sum(run_sum, axis=-1, keepdims=True)
            outs.append(_dot(p_ref[m, :, :n_keys], v_ref[:n_keys, :]) / denom)
        rows = slice(qi * tile, (qi + 1) * tile)
        if mode == "diff":
            o = outs[0] - lam * outs[1]
            o_ref[rows, :] = (_rms(o) * gain_ref[...] * (1.0 - lam_init)).astype(o_ref.dtype)
        else:
            o_ref[rows, :] = jnp.where(lane < half, outs[0], outs[1]).astype(o_ref.dtype)


def _prompt_attention(mode, rel_bias, bias_tiles, q, kb, vb, batch, seq, width_a, extras, lam_init=0.0):
    tile = ATT_TILE
    n_q = seq // tile
    n_heads_a = width_a // (2 * D_A)
    if mode == "diff":
        n_groups = width_a // LANES
        col0 = 0
        cols = (lambda g: g, lambda g: n_heads_a + g)
    else:
        n_groups = (q.shape[1] - width_a) // LANES
        col0 = width_a // LANES
        cols = (lambda g: 2 * n_heads_a + 2 * g, lambda g: 2 * n_heads_a + 2 * g + 1)
    seq_spec = pl.BlockSpec((seq, LANES), lambda b, g: (b, col0 + g))
    in_specs = [
        pl.BlockSpec(memory_space=pltpu.SMEM),
        seq_spec, seq_spec, seq_spec,
        pl.BlockSpec((None, 2, tile, tile), lambda b, g: (cols[0](g), 0, 0, 0)),
        pl.BlockSpec((None, 2, tile, tile), lambda b, g: (cols[1](g), 0, 0, 0)),
    ]
    args = [rel_bias, q, kb, vb, bias_tiles, bias_tiles]
    if mode == "diff":
        lq1, lk1, lq2, lk2, gain = extras
        in_specs += [pl.BlockSpec((1, D_A), lambda b, g: (0, 0))] * 4
        in_specs += [pl.BlockSpec((1, LANES), lambda b, g: (0, 0))]
        args += [lq1, lk1, lq2, lk2, gain]
    else:
        (kmean,) = extras
        n_blk = kmean.shape[1]
        in_specs += [pl.BlockSpec((None, n_blk, LANES), lambda b, g: (b, 0, g))]
        args += [kmean]
    return pl.pallas_call(
        functools.partial(_prompt_attn_kernel, mode=mode, tile=tile, n_q=n_q, n_heads_a=n_heads_a,
                          lam_init=lam_init),
        out_shape=jax.ShapeDtypeStruct((batch * seq, n_groups * LANES), BF16),
        grid=(batch, n_groups),
        in_specs=in_specs,
        out_specs=pl.BlockSpec((seq, LANES), lambda b, g: (b, g)),
        scratch_shapes=[pltpu.VMEM((2, n_q, tile, tile), F32), pltpu.VMEM((2, tile, seq), BF16)],
        compiler_params=_params("parallel", "parallel"),
        name="prompt_" + mode,
    )(*args)


def _block_mean_kernel(k_ref, o_ref):
    o_ref[...] = jnp.mean(k_ref[...], axis=0, keepdims=True)


def _prompt_block_means(k, width_a, width_b):
    assert width_a % width_b == 0
    n_blk = k.shape[0] // MOBA_BLOCK
    return pl.pallas_call(
        _block_mean_kernel,
        out_shape=jax.ShapeDtypeStruct((n_blk, 1, width_b), F32),
        grid=(n_blk,),
        in_specs=[pl.BlockSpec((MOBA_BLOCK, width_b), lambda i: (i, width_a // width_b))],
        out_specs=pl.BlockSpec((None, 1, width_b), lambda i: (i, 0, 0)),
        compiler_params=_params("parallel"),
        name="prompt_block_means",
    )(k)


def _block_diag_queries(q, n_groups, group_width):
    lane = lax.broadcasted_iota(jnp.int32, q.shape, 1)
    rows = [jnp.where((lane >= g * group_width) & (lane < (g + 1) * group_width), q, jnp.zeros_like(q))
            for g in range(n_groups)]
    return jnp.concatenate(rows, axis=0)


def _gather_block_diag(x, n_groups, rows_per_group, group_width):
    lane = lax.broadcasted_iota(jnp.int32, (rows_per_group, x.shape[1]), 1)
    out = jnp.zeros((rows_per_group, x.shape[1]), x.dtype)
    for g in range(n_groups):
        part = x[g * rows_per_group:(g + 1) * rows_per_group, :]
        out = jnp.where((lane >= g * group_width) & (lane < (g + 1) * group_width), part, out)
    return out


def _pad_rows(x, n_rows):
    if x.shape[0] == n_rows:
        return x
    return jnp.concatenate([x, jnp.zeros((n_rows - x.shape[0], x.shape[1]), x.dtype)], axis=0)


def _softmax_start(s, v, m_ref, l_ref, acc_ref):
    mx = jnp.max(s, axis=-1, keepdims=True)
    p = jnp.exp(s - mx)
    m_ref[...] = mx
    l_ref[...] = jnp.sum(p, axis=-1, keepdims=True)
    acc_ref[...] = _dot(p.astype(BF16), v)


def _softmax_step(s, v, m_ref, l_ref, acc_ref):
    m_old = m_ref[...]
    mx = jnp.maximum(m_old, jnp.max(s, axis=-1, keepdims=True))
    alpha = jnp.exp(m_old - mx)
    p = jnp.exp(s - mx)
    l_ref[...] = alpha * l_ref[...] + jnp.sum(p, axis=-1, keepdims=True)
    acc_ref[...] = alpha * acc_ref[...] + _dot(p.astype(BF16), v)
    m_ref[...] = mx


def _sample_diff_kernel(pt_ref, *refs, n_pages_step, page, dec, n_heads, lam_init):
    k_refs = refs[:n_pages_step]
    v_refs = refs[n_pages_step:2 * n_pages_step]
    (q_ref, kn_ref, vn_ref, bias_ref, lq1_ref, lk1_ref, lq2_ref, lk2_ref, gain_ref,
     o_ref, qb_ref, m_ref, l_ref, acc_ref) = refs[2 * n_pages_step:]
    c = pl.program_id(1)
    n_steps = pl.num_programs(1)

    @pl.when(c == 0)
    def _():
        qb_ref[...] = _block_diag_queries(q_ref[...].astype(F32), 2 * n_heads, D_A).astype(BF16)
        k_new = _pad_rows(kn_ref[...], LANES).astype(BF16)
        v_new = _pad_rows(vn_ref[...], LANES).astype(BF16)
        s = _dot_nt(qb_ref[...], k_new) + bias_ref[2]
        _softmax_start(s, v_new, m_ref, l_ref, acc_ref)

    qb = qb_ref[...]
    scores = []
    for p in range(n_pages_step):
        s = _dot_nt(qb, k_refs[p][...].astype(BF16))
        if p == n_pages_step - 1:
            s = s + jnp.where(c == n_steps - 1, bias_ref[0], bias_ref[1])
        else:
            s = s + bias_ref[1]
        scores.append(s)
    s = jnp.concatenate(scores, axis=1)
    v = jnp.concatenate([v_refs[p][...].astype(BF16) for p in range(n_pages_step)], axis=0)
    _softmax_step(s, v, m_ref, l_ref, acc_ref)

    @pl.when(c == n_steps - 1)
    def _():
        on = acc_ref[...] / l_ref[...]
        lane = lax.broadcasted_iota(jnp.int32, (dec, on.shape[1]), 1)
        o1 = jnp.zeros((dec, on.shape[1]), F32)
        o2 = jnp.zeros((dec, on.shape[1]), F32)
        for h in range(n_heads):
            in_head = (lane >= h * 2 * D_A) & (lane < (h + 1) * 2 * D_A)
            o1 = jnp.where(in_head, on[(2 * h) * dec:(2 * h + 1) * dec, :], o1)
            o2 = jnp.where(in_head, on[(2 * h + 1) * dec:(2 * h + 2) * dec, :], o2)
        lam = _lambda_value(lq1_ref, lk1_ref, lq2_ref, lk2_ref, lam_init)
        o = o1 - lam * o2
        for h in range(n_heads):
            cols = slice(h * 2 * D_A, (h + 1) * 2 * D_A)
            o_ref[:, cols] = (_rms(o[:, cols]) * gain_ref[...] * (1.0 - lam_init)).astype(o_ref.dtype)


def _sample_diff(page_table, cache_k, cache_v, layer, q, k_new, v_new, bias, lam_vecs, gain, width_a, lam_init):
    n_seq, n_pages = page_table.shape
    page = cache_k.shape[2]
    dec = q.shape[1]
    n_heads = width_a // (2 * D_A)
    pps = SAMPLE_DIFF_PAGES if n_pages % SAMPLE_DIFF_PAGES == 0 else 1
    n_steps = n_pages // pps
    n_rows = 2 * n_heads * dec
    assert page == LANES and 2 * D_A == LANES and bias.shape == (3, n_rows, LANES)

    def page_spec(p):
        return pl.BlockSpec((None, None, page, width_a),
                            lambda b, c, pt: (pt[b * n_pages + c * pps + p], layer, 0, 0))

    row_spec = pl.BlockSpec((None, dec, width_a), lambda b, c, pt: (b, 0, 0))
    vec_spec = pl.BlockSpec((1, D_A), lambda b, c, pt: (0, 0))
    in_specs = ([page_spec(p) for p in range(pps)] * 2
                + [row_spec, row_spec, row_spec,
                   pl.BlockSpec((3, n_rows, LANES), lambda b, c, pt: (0, 0, 0)),
                   vec_spec, vec_spec, vec_spec, vec_spec,
                   pl.BlockSpec((1, LANES), lambda b, c, pt: (0, 0))])
    return pl.pallas_call(
        functools.partial(_sample_diff_kernel, n_pages_step=pps, page=page, dec=dec, n_heads=n_heads,
                          lam_init=lam_init),
        out_shape=jax.ShapeDtypeStruct((n_seq, dec, width_a), BF16),
        grid_spec=pltpu.PrefetchScalarGridSpec(
            num_scalar_prefetch=1,
            grid=(n_seq, n_steps),
            in_specs=in_specs,
            out_specs=pl.BlockSpec((None, dec, width_a), lambda b, c, pt: (b, 0, 0)),
            scratch_shapes=[pltpu.VMEM((n_rows, width_a), BF16), pltpu.VMEM((n_rows, 1), F32),
                            pltpu.VMEM((n_rows, 1), F32), pltpu.VMEM((n_rows, width_a), F32)]),
        compiler_params=_params("parallel", "arbitrary"),
        name="sample_diff",
    )(page_table.reshape(-1), *([cache_k] * pps), *([cache_v] * pps), q, k_new, v_new, bias, *lam_vecs, gain)


def _sample_moba_kernel(pt_ref, *refs, n_seq, n_pages_step, n_steps, dec, n_heads, n_blk, pages_per_block):
    k_refs = refs[:n_pages_step]
    v_refs = refs[n_pages_step:2 * n_pages_step]
    (q_ref, kn_ref, vn_ref, bias_ref,
     o_ref, qb_ref, s_ref, smax_ref, kmean_ref, sel_ref, m_ref, l_ref, acc_ref) = refs[2 * n_pages_step:]
    b = pl.program_id(0)
    c = pl.program_id(1)
    in_key_phase = b < n_seq
    in_value_phase = b >= 1
    page = k_refs[0].shape[0]
    blocks_step = n_pages_step // pages_per_block
    blk_lane = lax.broadcasted_iota(jnp.int32, smax_ref.shape, 1)

    @pl.when(jnp.logical_and(in_value_phase, c == 0))
    def _():
        qb = qb_ref[...]
        parts = _dot_nt(qb, _bf16_split_rows([kmean_ref[...]], LANES))
        gate = parts[:, :LANES] + parts[:, LANES:2 * LANES] + parts[:, 2 * LANES:]
        gate = jnp.where(blk_lane < n_blk, gate, NEG_INF)
        sel = _topk_block_mask(gate, min(MOBA_TOPK, n_blk))
        sel_ref[...] = sel
        k_new = _pad_rows(kn_ref[...], LANES).astype(BF16)
        v_new = _pad_rows(vn_ref[...], LANES).astype(BF16)
        s_new = _dot_nt(qb, k_new) + bias_ref[2][:, :LANES]
        mx = jnp.maximum(jnp.max(s_new, axis=-1, keepdims=True),
                         jnp.max(smax_ref[...] + sel, axis=-1, keepdims=True))
        p = jnp.exp(s_new - mx)
        m_ref[...] = mx
        l_ref[...] = jnp.sum(p, axis=-1, keepdims=True)
        acc_ref[...] = _dot(p.astype(BF16), v_new)

    @pl.when(in_value_phase)
    def _():
        sel = sel_ref[...]
        mx = m_ref[...]
        parts = []
        run_sum = None
        for blk in range(blocks_step):
            picked = jnp.max(jnp.where(blk_lane == c * blocks_step + blk, sel, NEG_INF), axis=-1, keepdims=True)
            p = jnp.exp(s_ref[c, :, blk * MOBA_BLOCK:(blk + 1) * MOBA_BLOCK] + (picked - mx))
            run_sum = p if run_sum is None else run_sum + p
            parts.append(p.astype(BF16))
        p = jnp.concatenate(parts, axis=1)
        v = jnp.concatenate([v_refs[i][...].astype(BF16) for i in range(n_pages_step)], axis=0)
        l_ref[...] += jnp.sum(run_sum, axis=-1, keepdims=True)
        acc_ref[...] += _dot(p, v)

    @pl.when(jnp.logical_and(in_value_phase, c == n_steps - 1))
    def _():
        on = acc_ref[...] / l_ref[...]
        o_ref[...] = _gather_block_diag(on, n_heads, dec, D_B).astype(o_ref.dtype)

    @pl.when(jnp.logical_and(in_key_phase, c == 0))
    def _():
        qb_ref[...] = _block_diag_queries(q_ref[...].astype(F32), n_heads, D_B).astype(BF16)
        smax_ref[...] = jnp.full(smax_ref.shape, NEG_INF, F32)
        kmean_ref[...] = jnp.zeros_like(kmean_ref)

    @pl.when(in_key_phase)
    def _():
        qb = qb_ref[...]
        far = bias_ref[1][:, :page]
        smax = smax_ref[...]
        for blk in range(blocks_step):
            blk_max = None
            for p in range(blk * pages_per_block, (blk + 1) * pages_per_block):
                s = _dot_nt(qb, k_refs[p][...].astype(BF16))
                in_last_block = p - (n_pages_step - pages_per_block)
                if in_last_block >= 0:
                    near = bias_ref[0][:, in_last_block * page:(in_last_block + 1) * page]
                    s = s + jnp.where(c == n_steps - 1, near, far)
                else:
                    s = s + far
                s_ref[c, :, p * page:(p + 1) * page] = s
                blk_max = s if blk_max is None else jnp.maximum(blk_max, s)
            smax = jnp.where(blk_lane == c * blocks_step + blk, jnp.max(blk_max, axis=-1, keepdims=True), smax)
        smax_ref[...] = smax
        means = []
        for blk in range(blocks_step):
            total = k_refs[blk * pages_per_block][...]
            for p in range(1, pages_per_block):
                total = total + k_refs[blk * pages_per_block + p][...]
            means.append(jnp.sum(total, axis=0, keepdims=True) * (1.0 / MOBA_BLOCK))
        first_blk = pl.multiple_of(c * blocks_step, blocks_step)
        kmean_ref[pl.ds(first_blk, blocks_step), :] = jnp.concatenate(means, axis=0)


def _sample_moba(page_table, cache_k, cache_v, layer, q, k_new, v_new, bias, width_a, width_b):
    n_seq, n_pages = page_table.shape
    page = cache_k.shape[2]
    dec = q.shape[1]
    n_heads = width_b // D_B
    pps = SAMPLE_MOBA_PAGES
    ppb = MOBA_BLOCK // page
    n_blk = n_pages // ppb
    assert n_pages % pps == 0 and pps // ppb == SUBLANES and width_a % width_b == 0 and n_blk <= LANES
    n_steps = n_pages // pps
    n_rows = n_heads * dec
    col_blk = width_a // width_b
    assert bias.shape == (3, n_rows, MOBA_BLOCK)

    def key_spec(p):
        return pl.BlockSpec((None, None, page, width_b), lambda b, c, pt: (
            pt[jnp.minimum(b, n_seq - 1) * n_pages + jnp.where(b < n_seq, c, n_steps - 1) * pps + p],
            layer, 0, col_blk))

    def value_spec(p):
        return pl.BlockSpec((None, None, page, width_b), lambda b, c, pt: (
            pt[jnp.maximum(b - 1, 0) * n_pages + jnp.where(b >= 1, c, 0) * pps + p], layer, 0, col_blk))

    new_seq_spec = pl.BlockSpec((None, dec, width_b), lambda b, c, pt: (jnp.minimum(b, n_seq - 1), 0, col_blk))
    old_seq_spec = pl.BlockSpec((None, dec, width_b), lambda b, c, pt: (jnp.maximum(b - 1, 0), 0, col_blk))
    return pl.pallas_call(
        functools.partial(_sample_moba_kernel, n_seq=n_seq, n_pages_step=pps, n_steps=n_steps, dec=dec,
                          n_heads=n_heads, n_blk=n_blk, pages_per_block=ppb),
        out_shape=jax.ShapeDtypeStruct((n_seq, dec, width_b), BF16),
        grid_spec=pltpu.PrefetchScalarGridSpec(
            num_scalar_prefetch=1,
            grid=(n_seq + 1, n_steps),
            in_specs=[key_spec(p) for p in range(pps)] + [value_spec(p) for p in range(pps)]
            + [new_seq_spec, old_seq_spec, old_seq_spec,
               pl.BlockSpec((3, n_rows, MOBA_BLOCK), lambda b, c, pt: (0, 0, 0))],
            out_specs=pl.BlockSpec((None, dec, width_b), lambda b, c, pt: (jnp.maximum(b - 1, 0), 0, 0)),
            scratch_shapes=[pltpu.VMEM((n_rows, width_b), BF16),
                            pltpu.VMEM((n_steps, n_rows, pps * page), F32),
                            pltpu.VMEM((n_rows, LANES), F32),
                            pltpu.VMEM((LANES, width_b), F32),
                            pltpu.VMEM((n_rows, LANES), F32),
                            pltpu.VMEM((n_rows, 1), F32), pltpu.VMEM((n_rows, 1), F32),
                            pltpu.VMEM((n_rows, width_b), F32)]),
        compiler_params=_params("arbitrary", "arbitrary"),
        name="sample_moba",
    )(page_table.reshape(-1), *([cache_k] * pps), *([cache_v] * pps), q, k_new, v_new, bias)


def _merge_kernel(oa_ref, ob_ref, sa_ref, sb_ref, h_ref, wa_ref, wb_ref, wo_ref, o_ref):
    m = (sa_ref[...].astype(F32) * _dot(oa_ref[...], wa_ref[...])
         + sb_ref[...].astype(F32) * _dot(ob_ref[...], wb_ref[...]))
    o_ref[...] = h_ref[...] + _dot(m.astype(BF16), wo_ref[...])


def _merge(oa, ob, sa, sb, h, w_up_a, w_up_b, w_out):
    t, d = h.shape
    tm = _row_tile(t, 256)
    wa, wb = oa.shape[1], ob.shape[1]
    row = lambda w: pl.BlockSpec((tm, w), lambda i: (i, 0))
    full = lambda a: pl.BlockSpec(a.shape, lambda i: (0, 0))
    return pl.pallas_call(
        _merge_kernel,
        out_shape=jax.ShapeDtypeStruct((t, d), F32),
        grid=(t // tm,),
        in_specs=[row(wa), row(wb), row(d), row(d), row(d), full(w_up_a), full(w_up_b), full(w_out)],
        out_specs=row(d),
        compiler_params=_params("parallel"),
        name="merge",
    )(oa, ob, sa, sb, h, w_up_a, w_up_b, w_out)


def _pad_cols(w, mult):
    pad = (-w.shape[1]) % mult
    return jnp.pad(w, ((0, 0), (0, pad)))


def _pad_rows_to(w, mult):
    pad = (-w.shape[0]) % mult
    return jnp.pad(w, ((0, pad), (0, 0)))


def kernel(x_prompt, x_sample, cache_k, cache_v, page_table, rel_bias, ffn1_norm, ffn1_w_gate, ffn1_w_up,
           ffn1_w_down, mix_norm, w_in, lambda_q1, lambda_k1, lambda_q2, lambda_k2, subln_gain, w_up_a,
           w_up_b, w_out, ffn2_norm, ffn2_w_gate, ffn2_w_up, ffn2_w_down, final_norm):
    batch, seq, d_model = x_prompt.shape
    n_seq, dec, _ = x_sample.shape
    depth = w_in.shape[0]
    width_a = w_up_a.shape[1]
    width_b = w_up_b.shape[1]
    kv_width = width_a + width_b
    n_heads_a = width_a // (2 * D_A)
    n_heads_b = width_b // D_B
    page = cache_k.shape[2]
    assert seq % ATT_TILE == 0 and ATT_TILE >= FAR_DISTANCE and page >= FAR_DISTANCE
    assert rel_bias.shape == (N_BUCKETS, 2 * n_heads_a + n_heads_b)

    hp = x_prompt.reshape(batch * seq, d_model)
    hs = x_sample.reshape(n_seq * dec, d_model)
    row = lambda a: a.reshape(1, -1)

    prompt_tiles = _prompt_bias_tiles(rel_bias, ATT_TILE)
    diff_tiles = _sample_bias_tiles(rel_bias, 2 * n_heads_a, lambda g: (g % 2) * n_heads_a + g // 2, dec, page)
    moba_tiles = _sample_bias_tiles(rel_bias, n_heads_b, lambda g: 2 * n_heads_a + g, dec, MOBA_BLOCK)

    k_p, v_p, k_s, v_s = [], [], [], []
    for l in range(depth):
        lam_init = 0.8 - 0.6 * math.exp(-0.3 * l)
        lam_vecs = [row(a[l]) for a in (lambda_q1, lambda_k1, lambda_q2, lambda_k2)]
        gain = row(subln_gain[l])

        def ffn_weights(wg, wu, wd):
            return (_pad_cols(wg[l], FFN_COL_TILE).astype(BF16), _pad_cols(wu[l], FFN_COL_TILE).astype(BF16),
                    _pad_rows_to(wd[l], FFN_COL_TILE).astype(BF16))

        w1 = ffn_weights(ffn1_w_gate, ffn1_w_up, ffn1_w_down)
        hp = _ffn(hp, row(ffn1_norm[l]), *w1)
        hs = _ffn(hs, row(ffn1_norm[l]), *w1)

        w_in_b = w_in[l].astype(BF16)
        qp, kp, vp, kpb, vpb, sap, sbp = _project(hp, row(mix_norm[l]), w_in_b, kv_width)
        qs, ks, vs, _, _, sas, sbs = _project(hs, row(mix_norm[l]), w_in_b, kv_width)

        oa_p = _prompt_attention("diff", rel_bias, prompt_tiles, qp, kpb, vpb, batch, seq, width_a,
                                 lam_vecs + [gain], lam_init)
        kmean_p = _prompt_block_means(kp, width_a, width_b).reshape(batch, seq // MOBA_BLOCK, width_b)
        ob_p = _prompt_attention("moba", rel_bias, prompt_tiles, qp, kpb, vpb, batch, seq, width_a, [kmean_p])

        qs3 = qs.reshape(n_seq, dec, kv_width)
        ks3 = ks.reshape(n_seq, dec, kv_width)
        vs3 = vs.reshape(n_seq, dec, kv_width)
        oa_s = _sample_diff(page_table, cache_k, cache_v, l, qs3, ks3, vs3, diff_tiles, lam_vecs, gain,
                            width_a, lam_init)
        ob_s = _sample_moba(page_table, cache_k, cache_v, l, qs3, ks3, vs3, moba_tiles, width_a, width_b)

        wa_b, wb_b, wo_b = w_up_a[l].astype(BF16), w_up_b[l].astype(BF16), w_out[l].astype(BF16)
        hp = _merge(oa_p, ob_p, sap, sbp, hp, wa_b, wb_b, wo_b)
        hs = _merge(oa_s.reshape(n_seq * dec, width_a), ob_s.reshape(n_seq * dec, width_b), sas, sbs, hs,
                    wa_b, wb_b, wo_b)

        w2 = ffn_weights(ffn2_w_gate, ffn2_w_up, ffn2_w_down)
        last = l == depth - 1
        fg = row(final_norm) if last else None
        hp = _ffn(hp, row(ffn2_norm[l]), *w2, final_g=fg)
        hs = _ffn(hs, row(ffn2_norm[l]), *w2, final_g=fg)

        k_p.append(kp.reshape(batch, seq, kv_width))
        v_p.append(vp.reshape(batch, seq, kv_width))
        k_s.append(ks3)
        v_s.append(vs3)

    return (hp.reshape(batch, seq, d_model), hs.reshape(n_seq, dec, d_model),
            jnp.stack(k_p, axis=1), jnp.stack(v_p, axis=1), jnp.stack(k_s, axis=1), jnp.stack(v_s, axis=1))
```

```python
import functools
import math

import numpy as np
import jax
import jax.numpy as jnp
from jax import lax
from jax.experimental import pallas as pl
from jax.experimental.pallas import tpu as pltpu

F32 = jnp.float32
BF16 = jnp.bfloat16
NEG_INF = float("-inf")

D_A = 64
D_B = 64
MOBA_BLOCK = 256
MOBA_TOPK = 3
N_BUCKETS = 32
MAX_DISTANCE = 128
EPS = 1e-6
QK_SCALE = D_A ** -0.5

LANES = 128
SUBLANES = 8
VMEM_LIMIT_BYTES = 56 * 1024 * 1024

ATT_TILE = MOBA_BLOCK
FFN_COL_TILE = 512
PROJ_COL_TILE = 512
SAMPLE_DIFF_PAGES = 16
SAMPLE_MOBA_PAGES = 16


def _bucket_thresholds():
    n = np.arange(0, 4 * MAX_DISTANCE, dtype=np.int32)
    max_exact = N_BUCKETS // 2
    nf = np.maximum(n, max_exact).astype(np.float32)
    log_b = max_exact + (np.log(nf / np.float32(max_exact)) / np.float32(math.log(MAX_DISTANCE / max_exact))
                         * np.float32(N_BUCKETS - max_exact)).astype(np.int32)
    bucket = np.where(n < max_exact, n, np.minimum(log_b, N_BUCKETS - 1))
    assert bucket[-1] == N_BUCKETS - 1
    return [int(np.argmax(bucket >= b)) for b in range(1, N_BUCKETS)]


BUCKET_THR = _bucket_thresholds()
FAR_DISTANCE = BUCKET_THR[-1]


def _params(*sem):
    return pltpu.CompilerParams(dimension_semantics=sem, vmem_limit_bytes=VMEM_LIMIT_BYTES)


def _rms(x):
    return x * lax.rsqrt(jnp.mean(x * x, axis=-1, keepdims=True) + EPS)


def _dot(a, b):
    return jnp.dot(a, b, preferred_element_type=F32)


def _dot_nt(a, b, precision=None):
    return lax.dot_general(a, b, (((1,), (1,)), ((), ())), preferred_element_type=F32, precision=precision)


def _row_tile(n, pref):
    t = min(n, pref)
    assert n % t == 0
    return t


def _ffn_kernel(x_ref, g_ref, wg_ref, wu_ref, wd_ref, *rest, n_col, final):
    if final:
        fg_ref, o_ref, hn_ref = rest
    else:
        o_ref, hn_ref = rest
    j = pl.program_id(1)

    @pl.when(j == 0)
    def _():
        hn_ref[...] = (_rms(x_ref[...]) * g_ref[...]).astype(BF16)
        o_ref[...] = jnp.zeros_like(o_ref)

    hn = hn_ref[...]
    a = _dot(hn, wg_ref[...])
    b = _dot(hn, wu_ref[...])
    act = (a * jax.nn.sigmoid(a) * b).astype(BF16)
    o_ref[...] += _dot(act, wd_ref[...])

    @pl.when(j == n_col - 1)
    def _():
        h = x_ref[...] + 0.5 * o_ref[...]
        if final:
            h = _rms(h) * fg_ref[...]
        o_ref[...] = h


def _ffn(x, g, wg, wu, wd, final_g=None):
    t, d = x.shape
    f = wg.shape[1]
    tm = _row_tile(t, 512)
    tf = FFN_COL_TILE
    n_col = f // tf
    final = final_g is not None
    in_specs = [
        pl.BlockSpec((tm, d), lambda i, j: (i, 0)),
        pl.BlockSpec((1, d), lambda i, j: (0, 0)),
        pl.BlockSpec((d, tf), lambda i, j: (0, j)),
        pl.BlockSpec((d, tf), lambda i, j: (0, j)),
        pl.BlockSpec((tf, d), lambda i, j: (j, 0)),
    ]
    args = [x, g, wg, wu, wd]
    if final:
        in_specs.append(pl.BlockSpec((1, d), lambda i, j: (0, 0)))
        args.append(final_g)
    return pl.pallas_call(
        functools.partial(_ffn_kernel, n_col=n_col, final=final),
        out_shape=jax.ShapeDtypeStruct((t, d), F32),
        grid=(t // tm, n_col),
        in_specs=in_specs,
        out_specs=pl.BlockSpec((tm, d), lambda i, j: (i, 0)),
        scratch_shapes=[pltpu.VMEM((tm, d), BF16)],
        compiler_params=_params("parallel", "arbitrary"),
        name="ffn_final" if final else "ffn",
    )(*args)


def _proj_kernel(h_ref, g_ref, wq_ref, wk_ref, wv_ref, wa_ref, wb_ref,
                 q_ref, k_ref, v_ref, kb_ref, vb_ref, sa_ref, sb_ref, hn_ref):
    @pl.when(pl.program_id(1) == 0)
    def _():
        hn_ref[...] = (_rms(h_ref[...]) * g_ref[...]).astype(BF16)

    hn = hn_ref[...]
    q_ref[...] = (_dot(hn, wq_ref[...]) * QK_SCALE).astype(BF16)
    k = _dot(hn, wk_ref[...])
    k_ref[...] = k
    kb_ref[...] = k.astype(BF16)
    v = _dot(hn, wv_ref[...])
    v_ref[...] = v
    vb_ref[...] = v.astype(BF16)
    sa_ref[...] = jax.nn.sigmoid(_dot(hn, wa_ref[...])).astype(BF16)
    sb_ref[...] = jax.nn.sigmoid(_dot(hn, wb_ref[...])).astype(BF16)


def _project(h, g, w_in, kv_width):
    t, d = h.shape
    assert kv_width == d and w_in.shape[1] == 5 * d
    tm = _row_tile(t, 512)
    tn = _row_tile(d, PROJ_COL_TILE)
    n_col = d // tn
    w_specs = [pl.BlockSpec((d, tn), functools.partial(lambda i, j, c: (0, c * n_col + j), c=c))
               for c in range(5)]
    out_spec = pl.BlockSpec((tm, tn), lambda i, j: (i, j))
    out_shape = [jax.ShapeDtypeStruct((t, d), dt) for dt in (BF16, F32, F32, BF16, BF16, BF16, BF16)]
    return pl.pallas_call(
        _proj_kernel,
        out_shape=out_shape,
        grid=(t // tm, n_col),
        in_specs=[pl.BlockSpec((tm, d), lambda i, j: (i, 0)),
                  pl.BlockSpec((1, d), lambda i, j: (0, 0))] + w_specs,
        out_specs=[out_spec] * 7,
        scratch_shapes=[pltpu.VMEM((tm, d), BF16)],
        compiler_params=_params("parallel", "arbitrary"),
        name="project",
    )(h, g, w_in, w_in, w_in, w_in, w_in)


def _bias_of_distance(d, tab_ref, col):
    v = jnp.full(d.shape, tab_ref[0, col], F32)
    for b in range(1, N_BUCKETS):
        if b < N_BUCKETS - 1 and BUCKET_THR[b - 1] == BUCKET_THR[b]:
            continue
        v = jnp.where(d >= BUCKET_THR[b - 1], tab_ref[b, col], v)
    return v


def _prompt_bias_kernel(tab_ref, o_ref, *, tile):
    col = pl.program_id(0)
    i = lax.broadcasted_iota(jnp.int32, (tile, tile), 0)
    j = lax.broadcasted_iota(jnp.int32, (tile, tile), 1)
    o_ref[0] = _bias_of_distance(i - j, tab_ref, col)
    o_ref[1] = _bias_of_distance(i - j + tile, tab_ref, col)


def _prompt_bias_tiles(rel_bias, tile):
    n_col = rel_bias.shape[1]
    return pl.pallas_call(
        functools.partial(_prompt_bias_kernel, tile=tile),
        out_shape=jax.ShapeDtypeStruct((n_col, 2, tile, tile), F32),
        grid=(n_col,),
        in_specs=[pl.BlockSpec(memory_space=pltpu.SMEM)],
        out_specs=pl.BlockSpec((None, 2, tile, tile), lambda c: (c, 0, 0, 0)),
        compiler_params=_params("parallel"),
        name="prompt_bias_tiles",
    )(rel_bias)


def _sample_bias_kernel(tab_ref, o_ref, *, n_groups, col_of_group, dec, last_width):
    width = o_ref.shape[-1]
    i = lax.broadcasted_iota(jnp.int32, (dec, width), 0)
    j = lax.broadcasted_iota(jnp.int32, (dec, width), 1)
    for g in range(n_groups):
        col = col_of_group(g)
        rows = slice(g * dec, (g + 1) * dec)
        o_ref[0, rows, :] = _bias_of_distance(last_width + i - j, tab_ref, col)
        o_ref[1, rows, :] = jnp.full((dec, width), tab_ref[N_BUCKETS - 1, col], F32)
        new = _bias_of_distance(i - j, tab_ref, col)
        o_ref[2, rows, :] = jnp.where((j <= i) & (j < dec), new, NEG_INF)


def _sample_bias_tiles(rel_bias, n_groups, col_of_group, dec, last_width):
    width = max(last_width, LANES)
    return pl.pallas_call(
        functools.partial(_sample_bias_kernel, n_groups=n_groups, col_of_group=col_of_group, dec=dec,
                          last_width=last_width),
        out_shape=jax.ShapeDtypeStruct((3, n_groups * dec, width), F32),
        in_specs=[pl.BlockSpec(memory_space=pltpu.SMEM)],
        name="sample_bias_tiles",
    )(rel_bias)


def _lambda_value(lq1_ref, lk1_ref, lq2_ref, lk2_ref, lam_init):
    e1 = jnp.exp(jnp.sum(lq1_ref[...] * lk1_ref[...], axis=-1, keepdims=True))
    e2 = jnp.exp(jnp.sum(lq2_ref[...] * lk2_ref[...], axis=-1, keepdims=True))
    return e1 - e2 + lam_init


def _topk_block_mask(gate, k):
    n = gate.shape[-1]
    lane = lax.broadcasted_iota(jnp.int32, gate.shape, 1).astype(F32)
    mask = jnp.full(gate.shape, NEG_INF, F32)
    for _ in range(k):
        mx = jnp.max(gate, axis=-1, keepdims=True)
        first = jnp.min(jnp.where(gate == mx, lane, float(n)), axis=-1, keepdims=True)
        hit = lane == first
        mask = jnp.where(hit & (mx > NEG_INF), 0.0, mask)
        gate = jnp.where(hit, NEG_INF, gate)
    return mask


def _bf16_split_rows(mats, n_pad):
    parts = []
    for x in mats:
        rest = x
        for _ in range(3):
            piece = rest.astype(BF16).astype(F32)
            parts.append(_pad_rows(piece, n_pad))
            rest = rest - piece
    return jnp.concatenate(parts, axis=0).astype(BF16)


def _topk_rank_mask(gate, n, k):
    row_id = lax.broadcasted_iota(jnp.int32, gate.shape, 0)
    rank = jnp.zeros(gate.shape, F32)
    for other in range(n):
        cand = gate[other:other + 1, :]
        wins_tie = jnp.where(row_id > other, 1.0, 0.0)
        rank = rank + jnp.where(cand > gate, 1.0, jnp.where(cand == gate, wins_tie, 0.0))
    return jnp.where(rank < float(k), jnp.where(gate > NEG_INF, 0.0, NEG_INF), NEG_INF)


def _prompt_attn_kernel(*refs, mode, tile, n_q, n_heads_a, lam_init, n_side):
    n_main = 11 if mode == "diff" else 7
    side_in = refs[n_main:n_main + n_side]
    side_out = refs[n_main + n_side + 1:n_main + 2 * n_side + 1]
    o_ref = refs[n_main + n_side]
    s_ref, p_ref = refs[n_main + 2 * n_side + 1:]
    if mode == "diff":
        (tab_ref, q_ref, k_ref, v_ref, b1_ref, b2_ref, lq1_ref, lk1_ref, lq2_ref, lk2_ref, gain_ref) = refs[:n_main]
    else:
        (tab_ref, q_ref, k_ref, v_ref, b1_ref, b2_ref, kmean_ref) = refs[:n_main]
    for w_ref, wb_ref in zip(side_in, side_out):
        width = w_ref.shape[1]
        wb_ref[:, :width] = w_ref[...].astype(BF16)
        if wb_ref.shape[1] > width:
            wb_ref[:, width:] = jnp.zeros((wb_ref.shape[0], wb_ref.shape[1] - width), BF16)
    g = pl.program_id(1)
    half = LANES // 2
    lane = lax.broadcasted_iota(jnp.int32, (1, LANES), 1)
    bias_refs = (b1_ref, b2_ref)
    if mode == "diff":
        cols = (g, n_heads_a + g)
        lam = _lambda_value(lq1_ref, lk1_ref, lq2_ref, lk2_ref, lam_init)
    else:
        cols = (2 * n_heads_a + 2 * g, 2 * n_heads_a + 2 * g + 1)
        n_blk = kmean_ref.shape[0]
        n_pad = -(-n_blk // SUBLANES) * SUBLANES
        assert 2 * n_pad <= LANES
        kmean = kmean_ref[...]
        kmean_maps = (jnp.where(lane < half, kmean, 0.0), jnp.where(lane >= half, kmean, 0.0))
        gates = _dot_nt(_bf16_split_rows(kmean_maps, n_pad), q_ref[...])
        blk_id = lax.broadcasted_iota(jnp.int32, (n_pad, n_q * tile), 0)
        q_blk = lax.broadcasted_iota(jnp.int32, (n_pad, n_q * tile), 1) // tile
        sel_t = []
        for m in range(2):
            gate = gates[3 * m * n_pad:(3 * m + 1) * n_pad] + gates[(3 * m + 1) * n_pad:(3 * m + 2) * n_pad] \
                + gates[(3 * m + 2) * n_pad:(3 * m + 3) * n_pad]
            gate = jnp.where(blk_id < q_blk, gate, NEG_INF)
            sel_t.append(_topk_rank_mask(gate, n_blk, min(MOBA_TOPK, n_blk)))
    far = [tab_ref[N_BUCKETS - 1, cols[m]] for m in range(2)]
    row = lax.broadcasted_iota(jnp.int32, (tile, tile), 0)
    colv = lax.broadcasted_iota(jnp.int32, (tile, tile), 1)
    causal = row >= colv

    for qi in range(n_q):
        q = q_ref[qi * tile:(qi + 1) * tile, :]
        q_maps = (jnp.where(lane < half, q, jnp.zeros_like(q)), jnp.where(lane >= half, q, jnp.zeros_like(q)))
        if mode == "moba" and qi > 0:
            packed = [sel_t[m][:, qi * tile:(qi + 1) * tile] for m in range(2)]
            packed.append(jnp.zeros((LANES - 2 * n_pad, tile), F32))
            sel_q = jnp.concatenate(packed, axis=0).T
        outs = []
        for m in range(2):
            run_max = None
            for j in range(qi + 1):
                s = _dot_nt(q_maps[m], k_ref[j * tile:(j + 1) * tile, :])
                if j == qi:
                    s = jnp.where(causal, s + bias_refs[m][0], NEG_INF)
                else:
                    shift = bias_refs[m][1] if j == qi - 1 else far[m]
                    if mode == "moba":
                        shift = shift + sel_q[:, m * n_pad + j:m * n_pad + j + 1]
                    s = s + shift
                s_ref[m, j] = s
                run_max = s if run_max is None else jnp.maximum(run_max, s)
            mx = jnp.max(run_max, axis=-1, keepdims=True)
            run_sum = None
            for j in range(qi + 1):
                p = jnp.exp(s_ref[m, j] - mx)
                run_sum = p if run_sum is None else run_sum + p
                p_ref[m, :, j * tile:(j + 1) * tile] = p.astype(BF16)
            n_keys = (qi + 1) * tile
            denom = jnp.sum(run_sum, axis=-1, keepdims=True)
            outs.append(_dot(p_ref[m, :, :n_keys], v_ref[:n_keys, :]) / denom)
        rows = slice(qi * tile, (qi + 1) * tile)
        if mode == "diff":
            o = outs[0] - lam * outs[1]
            o_ref[rows, :] = (_rms(o) * gain_ref[...] * (1.0 - lam_init)).astype(o_ref.dtype)
        else:
            o_ref[rows, :] = jnp.where(lane < half, outs[0], outs[1]).astype(o_ref.dtype)


def _prompt_attention(mode, rel_bias, bias_tiles, q, kb, vb, batch, seq, width_a, extras, side_casts,
                      lam_init=0.0):
    tile = ATT_TILE
    n_q = seq // tile
    n_heads_a = width_a // (2 * D_A)
    if mode == "diff":
        n_groups = width_a // LANES
        col0 = 0
        cols = (lambda g: g, lambda g: n_heads_a + g)
    else:
        n_groups = (q.shape[1] - width_a) // LANES
        col0 = width_a // LANES
        cols = (lambda g: 2 * n_heads_a + 2 * g, lambda g: 2 * n_heads_a + 2 * g + 1)
    seq_spec = pl.BlockSpec((seq, LANES), lambda b, g: (b, col0 + g))
    in_specs = [
        pl.BlockSpec(memory_space=pltpu.SMEM),
        seq_spec, seq_spec, seq_spec,
        pl.BlockSpec((None, 2, tile, tile), lambda b, g: (cols[0](g), 0, 0, 0)),
        pl.BlockSpec((None, 2, tile, tile), lambda b, g: (cols[1](g), 0, 0, 0)),
    ]
    args = [rel_bias, q, kb, vb, bias_tiles, bias_tiles]
    if mode == "diff":
        lq1, lk1, lq2, lk2, gain = extras
        in_specs += [pl.BlockSpec((1, D_A), lambda b, g: (0, 0))] * 4
        in_specs += [pl.BlockSpec((1, LANES), lambda b, g: (0, 0))]
        args += [lq1, lk1, lq2, lk2, gain]
    else:
        (kmean,) = extras
        n_blk = kmean.shape[1]
        in_specs += [pl.BlockSpec((None, n_blk, LANES), lambda b, g: (b, 0, g))]
        args += [kmean]
    n_steps = batch * n_groups
    out_shape = [jax.ShapeDtypeStruct((batch * seq, n_groups * LANES), BF16)]
    out_specs = [pl.BlockSpec((seq, LANES), lambda b, g: (b, g))]
    for w, width_out in side_casts:
        rows = w.shape[0] // n_steps
        assert rows * n_steps == w.shape[0] and rows % (2 * SUBLANES) == 0 and width_out % LANES == 0
        in_specs.append(pl.BlockSpec((rows, w.shape[1]), lambda b, g: (b * n_groups + g, 0)))
        args.append(w)
        out_shape.append(jax.ShapeDtypeStruct((w.shape[0], width_out), BF16))
        out_specs.append(pl.BlockSpec((rows, width_out), lambda b, g: (b * n_groups + g, 0)))
    return pl.pallas_call(
        functools.partial(_prompt_attn_kernel, mode=mode, tile=tile, n_q=n_q, n_heads_a=n_heads_a,
                          lam_init=lam_init, n_side=len(side_casts)),
        out_shape=out_shape,
        grid=(batch, n_groups),
        in_specs=in_specs,
        out_specs=out_specs,
        scratch_shapes=[pltpu.VMEM((2, n_q, tile, tile), F32), pltpu.VMEM((2, tile, seq), BF16)],
        compiler_params=_params("parallel", "parallel"),
        name="prompt_" + mode,
    )(*args)


def _block_mean_kernel(k_ref, o_ref):
    o_ref[...] = jnp.mean(k_ref[...], axis=0, keepdims=True)


def _prompt_block_means(k, width_a, width_b):
    assert width_a % width_b == 0
    n_blk = k.shape[0] // MOBA_BLOCK
    return pl.pallas_call(
        _block_mean_kernel,
        out_shape=jax.ShapeDtypeStruct((n_blk, 1, width_b), F32),
        grid=(n_blk,),
        in_specs=[pl.BlockSpec((MOBA_BLOCK, width_b), lambda i: (i, width_a // width_b))],
        out_specs=pl.BlockSpec((None, 1, width_b), lambda i: (i, 0, 0)),
        compiler_params=_params("parallel"),
        name="prompt_block_means",
    )(k)


def _block_diag_queries(q, n_groups, group_width):
    lane = lax.broadcasted_iota(jnp.int32, q.shape, 1)
    rows = [jnp.where((lane >= g * group_width) & (lane < (g + 1) * group_width), q, jnp.zeros_like(q))
            for g in range(n_groups)]
    return jnp.concatenate(rows, axis=0)


def _gather_block_diag(x, n_groups, rows_per_group, group_width):
    lane = lax.broadcasted_iota(jnp.int32, (rows_per_group, x.shape[1]), 1)
    out = jnp.zeros((rows_per_group, x.shape[1]), x.dtype)
    for g in range(n_groups):
        part = x[g * rows_per_group:(g + 1) * rows_per_group, :]
        out = jnp.where((lane >= g * group_width) & (lane < (g + 1) * group_width), part, out)
    return out


def _pad_rows(x, n_rows):
    if x.shape[0] == n_rows:
        return x
    return jnp.concatenate([x, jnp.zeros((n_rows - x.shape[0], x.shape[1]), x.dtype)], axis=0)


def _softmax_start(s, v, m_ref, l_ref, acc_ref):
    mx = jnp.max(s, axis=-1, keepdims=True)
    p = jnp.exp(s - mx)
    m_ref[...] = mx
    l_ref[...] = jnp.sum(p, axis=-1, keepdims=True)
    acc_ref[...] = _dot(p.astype(BF16), v)


def _softmax_step(s, v, m_ref, l_ref, acc_ref):
    m_old = m_ref[...]
    mx = jnp.maximum(m_old, jnp.max(s, axis=-1, keepdims=True))
    alpha = jnp.exp(m_old - mx)
    p = jnp.exp(s - mx)
    l_ref[...] = alpha * l_ref[...] + jnp.sum(p, axis=-1, keepdims=True)
    acc_ref[...] = alpha * acc_ref[...] + _dot(p.astype(BF16), v)
    m_ref[...] = mx


def _sample_diff_kernel(pt_ref, *refs, n_pages_step, page, dec, n_heads, lam_init):
    k_refs = refs[:n_pages_step]
    v_refs = refs[n_pages_step:2 * n_pages_step]
    (q_ref, kn_ref, vn_ref, bias_ref, lq1_ref, lk1_ref, lq2_ref, lk2_ref, gain_ref,
     o_ref, qb_ref, m_ref, l_ref, acc_ref) = refs[2 * n_pages_step:]
    c = pl.program_id(1)
    n_steps = pl.num_programs(1)

    @pl.when(c == 0)
    def _():
        qb_ref[...] = _block_diag_queries(q_ref[...].astype(F32), 2 * n_heads, D_A).astype(BF16)
        k_new = _pad_rows(kn_ref[...], LANES).astype(BF16)
        v_new = _pad_rows(vn_ref[...], LANES).astype(BF16)
        s = _dot_nt(qb_ref[...], k_new) + bias_ref[2]
        _softmax_start(s, v_new, m_ref, l_ref, acc_ref)

    qb = qb_ref[...]
    scores = []
    for p in range(n_pages_step):
        s = _dot_nt(qb, k_refs[p][...].astype(BF16))
        if p == n_pages_step - 1:
            s = s + jnp.where(c == n_steps - 1, bias_ref[0], bias_ref[1])
        else:
            s = s + bias_ref[1]
        scores.append(s)
    s = jnp.concatenate(scores, axis=1)
    v = jnp.concatenate([v_refs[p][...].astype(BF16) for p in range(n_pages_step)], axis=0)
    _softmax_step(s, v, m_ref, l_ref, acc_ref)

    @pl.when(c == n_steps - 1)
    def _():
        on = acc_ref[...] / l_ref[...]
        lane = lax.broadcasted_iota(jnp.int32, (dec, on.shape[1]), 1)
        o1 = jnp.zeros((dec, on.shape[1]), F32)
        o2 = jnp.zeros((dec, on.shape[1]), F32)
        for h in range(n_heads):
            in_head = (lane >= h * 2 * D_A) & (lane < (h + 1) * 2 * D_A)
            o1 = jnp.where(in_head, on[(2 * h) * dec:(2 * h + 1) * dec, :], o1)
            o2 = jnp.where(in_head, on[(2 * h + 1) * dec:(2 * h + 2) * dec, :], o2)
        lam = _lambda_value(lq1_ref, lk1_ref, lq2_ref, lk2_ref, lam_init)
        o = o1 - lam * o2
        for h in range(n_heads):
            cols = slice(h * 2 * D_A, (h + 1) * 2 * D_A)
            o_ref[:, cols] = (_rms(o[:, cols]) * gain_ref[...] * (1.0 - lam_init)).astype(o_ref.dtype)


def _sample_diff(page_table, cache_k, cache_v, layer, q, k_new, v_new, bias, lam_vecs, gain, width_a, lam_init):
    n_seq, n_pages = page_table.shape
    page = cache_k.shape[2]
    dec = q.shape[1]
    n_heads = width_a // (2 * D_A)
    pps = SAMPLE_DIFF_PAGES if n_pages % SAMPLE_DIFF_PAGES == 0 else 1
    n_steps = n_pages // pps
    n_rows = 2 * n_heads * dec
    assert page == LANES and 2 * D_A == LANES and bias.shape == (3, n_rows, LANES)

    def page_spec(p):
        return pl.BlockSpec((None, None, page, width_a),
                            lambda b, c, pt: (pt[b * n_pages + c * pps + p], layer, 0, 0))

    row_spec = pl.BlockSpec((None, dec, width_a), lambda b, c, pt: (b, 0, 0))
    vec_spec = pl.BlockSpec((1, D_A), lambda b, c, pt: (0, 0))
    in_specs = ([page_spec(p) for p in range(pps)] * 2
                + [row_spec, row_spec, row_spec,
                   pl.BlockSpec((3, n_rows, LANES), lambda b, c, pt: (0, 0, 0)),
                   vec_spec, vec_spec, vec_spec, vec_spec,
                   pl.BlockSpec((1, LANES), lambda b, c, pt: (0, 0))])
    return pl.pallas_call(
        functools.partial(_sample_diff_kernel, n_pages_step=pps, page=page, dec=dec, n_heads=n_heads,
                          lam_init=lam_init),
        out_shape=jax.ShapeDtypeStruct((n_seq, dec, width_a), BF16),
        grid_spec=pltpu.PrefetchScalarGridSpec(
            num_scalar_prefetch=1,
            grid=(n_seq, n_steps),
            in_specs=in_specs,
            out_specs=pl.BlockSpec((None, dec, width_a), lambda b, c, pt: (b, 0, 0)),
            scratch_shapes=[pltpu.VMEM((n_rows, width_a), BF16), pltpu.VMEM((n_rows, 1), F32),
                            pltpu.VMEM((n_rows, 1), F32), pltpu.VMEM((n_rows, width_a), F32)]),
        compiler_params=_params("parallel", "arbitrary"),
        name="sample_diff",
    )(page_table.reshape(-1), *([cache_k] * pps), *([cache_v] * pps), q, k_new, v_new, bias, *lam_vecs, gain)


def _sample_moba_kernel(pt_ref, *refs, n_seq, n_pages_step, n_steps, dec, n_heads, n_blk, pages_per_block):
    k_refs = refs[:n_pages_step]
    v_refs = refs[n_pages_step:2 * n_pages_step]
    (q_ref, kn_ref, vn_ref, bias_ref,
     o_ref, qb_ref, s_ref, smax_ref, kmean_ref, sel_ref, m_ref, l_ref, acc_ref) = refs[2 * n_pages_step:]
    b = pl.program_id(0)
    c = pl.program_id(1)
    in_key_phase = b < n_seq
    in_value_phase = b >= 1
    page = k_refs[0].shape[0]
    blocks_step = n_pages_step // pages_per_block
    blk_lane = lax.broadcasted_iota(jnp.int32, smax_ref.shape, 1)

    @pl.when(jnp.logical_and(in_value_phase, c == 0))
    def _():
        qb = qb_ref[...]
        parts = _dot_nt(qb, _bf16_split_rows([kmean_ref[...]], LANES))
        gate = parts[:, :LANES] + parts[:, LANES:2 * LANES] + parts[:, 2 * LANES:]
        gate = jnp.where(blk_lane < n_blk, gate, NEG_INF)
        sel = _topk_block_mask(gate, min(MOBA_TOPK, n_blk))
        sel_ref[...] = sel
        k_new = _pad_rows(kn_ref[...], LANES).astype(BF16)
        v_new = _pad_rows(vn_ref[...], LANES).astype(BF16)
        s_new = _dot_nt(qb, k_new) + bias_ref[2][:, :LANES]
        mx = jnp.maximum(jnp.max(s_new, axis=-1, keepdims=True),
                         jnp.max(smax_ref[...] + sel, axis=-1, keepdims=True))
        p = jnp.exp(s_new - mx)
        m_ref[...] = mx
        l_ref[...] = jnp.sum(p, axis=-1, keepdims=True)
        acc_ref[...] = _dot(p.astype(BF16), v_new)

    @pl.when(in_value_phase)
    def _():
        sel = sel_ref[...]
        mx = m_ref[...]
        parts = []
        run_sum = None
        for blk in range(blocks_step):
            picked = jnp.max(jnp.where(blk_lane == c * blocks_step + blk, sel, NEG_INF), axis=-1, keepdims=True)
            p = jnp.exp(s_ref[c, :, blk * MOBA_BLOCK:(blk + 1) * MOBA_BLOCK] + (picked - mx))
            run_sum = p if run_sum is None else run_sum + p
            parts.append(p.astype(BF16))
        p = jnp.concatenate(parts, axis=1)
        v = jnp.concatenate([v_refs[i][...].astype(BF16) for i in range(n_pages_step)], axis=0)
        l_ref[...] += jnp.sum(run_sum, axis=-1, keepdims=True)
        acc_ref[...] += _dot(p, v)

    @pl.when(jnp.logical_and(in_value_phase, c == n_steps - 1))
    def _():
        on = acc_ref[...] / l_ref[...]
        o_ref[...] = _gather_block_diag(on, n_heads, dec, D_B).astype(o_ref.dtype)

    @pl.when(jnp.logical_and(in_key_phase, c == 0))
    def _():
        qb_ref[...] = _block_diag_queries(q_ref[...].astype(F32), n_heads, D_B).astype(BF16)
        smax_ref[...] = jnp.full(smax_ref.shape, NEG_INF, F32)
        kmean_ref[...] = jnp.zeros_like(kmean_ref)

    @pl.when(in_key_phase)
    def _():
        qb = qb_ref[...]
        far = bias_ref[1][:, :page]
        smax = smax_ref[...]
        for blk in range(blocks_step):
            blk_max = None
            for p in range(blk * pages_per_block, (blk + 1) * pages_per_block):
                s = _dot_nt(qb, k_refs[p][...].astype(BF16))
                in_last_block = p - (n_pages_step - pages_per_block)
                if in_last_block >= 0:
                    near = bias_ref[0][:, in_last_block * page:(in_last_block + 1) * page]
                    s = s + jnp.where(c == n_steps - 1, near, far)
                else:
                    s = s + far
                s_ref[c, :, p * page:(p + 1) * page] = s
                blk_max = s if blk_max is None else jnp.maximum(blk_max, s)
            smax = jnp.where(blk_lane == c * blocks_step + blk, jnp.max(blk_max, axis=-1, keepdims=True), smax)
        smax_ref[...] = smax
        means = []
        for blk in range(blocks_step):
            total = k_refs[blk * pages_per_block][...]
            for p in range(1, pages_per_block):
                total = total + k_refs[blk * pages_per_block + p][...]
            means.append(jnp.sum(total, axis=0, keepdims=True) * (1.0 / MOBA_BLOCK))
        first_blk = pl.multiple_of(c * blocks_step, blocks_step)
        kmean_ref[pl.ds(first_blk, blocks_step), :] = jnp.concatenate(means, axis=0)


def _sample_moba(page_table, cache_k, cache_v, layer, q, k_new, v_new, bias, width_a, width_b):
    n_seq, n_pages = page_table.shape
    page = cache_k.shape[2]
    dec = q.shape[1]
    n_heads = width_b // D_B
    pps = SAMPLE_MOBA_PAGES
    ppb = MOBA_BLOCK // page
    n_blk = n_pages // ppb
    assert n_pages % pps == 0 and pps // ppb == SUBLANES and width_a % width_b == 0 and n_blk <= LANES
    n_steps = n_pages // pps
    n_rows = n_heads * dec
    col_blk = width_a // width_b
    assert bias.shape == (3, n_rows, MOBA_BLOCK)

    def key_spec(p):
        return pl.BlockSpec((None, None, page, width_b), lambda b, c, pt: (
            pt[jnp.minimum(b, n_seq - 1) * n_pages + jnp.where(b < n_seq, c, n_steps - 1) * pps + p],
            layer, 0, col_blk))

    def value_spec(p):
        return pl.BlockSpec((None, None, page, width_b), lambda b, c, pt: (
            pt[jnp.maximum(b - 1, 0) * n_pages + jnp.where(b >= 1, c, 0) * pps + p], layer, 0, col_blk))

    new_seq_spec = pl.BlockSpec((None, dec, width_b), lambda b, c, pt: (jnp.minimum(b, n_seq - 1), 0, col_blk))
    old_seq_spec = pl.BlockSpec((None, dec, width_b), lambda b, c, pt: (jnp.maximum(b - 1, 0), 0, col_blk))
    return pl.pallas_call(
        functools.partial(_sample_moba_kernel, n_seq=n_seq, n_pages_step=pps, n_steps=n_steps, dec=dec,
                          n_heads=n_heads, n_blk=n_blk, pages_per_block=ppb),
        out_shape=jax.ShapeDtypeStruct((n_seq, dec, width_b), BF16),
        grid_spec=pltpu.PrefetchScalarGridSpec(
            num_scalar_prefetch=1,
            grid=(n_seq + 1, n_steps),
            in_specs=[key_spec(p) for p in range(pps)] + [value_spec(p) for p in range(pps)]
            + [new_seq_spec, old_seq_spec, old_seq_spec,
               pl.BlockSpec((3, n_rows, MOBA_BLOCK), lambda b, c, pt: (0, 0, 0))],
            out_specs=pl.BlockSpec((None, dec, width_b), lambda b, c, pt: (jnp.maximum(b - 1, 0), 0, 0)),
            scratch_shapes=[pltpu.VMEM((n_rows, width_b), BF16),
                            pltpu.VMEM((n_steps, n_rows, pps * page), F32),
                            pltpu.VMEM((n_rows, LANES), F32),
                            pltpu.VMEM((LANES, width_b), F32),
                            pltpu.VMEM((n_rows, LANES), F32),
                            pltpu.VMEM((n_rows, 1), F32), pltpu.VMEM((n_rows, 1), F32),
                            pltpu.VMEM((n_rows, width_b), F32)]),
        compiler_params=_params("arbitrary", "arbitrary"),
        name="sample_moba",
    )(page_table.reshape(-1), *([cache_k] * pps), *([cache_v] * pps), q, k_new, v_new, bias)


def _merge_kernel(oa_ref, ob_ref, sa_ref, sb_ref, h_ref, wa_ref, wb_ref, wo_ref, o_ref):
    m = (sa_ref[...].astype(F32) * _dot(oa_ref[...], wa_ref[...])
         + sb_ref[...].astype(F32) * _dot(ob_ref[...], wb_ref[...]))
    o_ref[...] = h_ref[...] + _dot(m.astype(BF16), wo_ref[...])


def _merge(oa, ob, sa, sb, h, w_up_a, w_up_b, w_out):
    t, d = h.shape
    tm = _row_tile(t, 256)
    wa, wb = oa.shape[1], ob.shape[1]
    row = lambda w: pl.BlockSpec((tm, w), lambda i: (i, 0))
    full = lambda a: pl.BlockSpec(a.shape, lambda i: (0, 0))
    return pl.pallas_call(
        _merge_kernel,
        out_shape=jax.ShapeDtypeStruct((t, d), F32),
        grid=(t // tm,),
        in_specs=[row(wa), row(wb), row(d), row(d), row(d), full(w_up_a), full(w_up_b), full(w_out)],
        out_specs=row(d),
        compiler_params=_params("parallel"),
        name="merge",
    )(oa, ob, sa, sb, h, w_up_a, w_up_b, w_out)


def _pad_cols(w, mult):
    pad = (-w.shape[1]) % mult
    return jnp.pad(w, ((0, 0), (0, pad)))


def _pad_rows_to(w, mult):
    pad = (-w.shape[0]) % mult
    return jnp.pad(w, ((0, pad), (0, 0)))


def kernel(x_prompt, x_sample, cache_k, cache_v, page_table, rel_bias, ffn1_norm, ffn1_w_gate, ffn1_w_up,
           ffn1_w_down, mix_norm, w_in, lambda_q1, lambda_k1, lambda_q2, lambda_k2, subln_gain, w_up_a,
           w_up_b, w_out, ffn2_norm, ffn2_w_gate, ffn2_w_up, ffn2_w_down, final_norm):
    batch, seq, d_model = x_prompt.shape
    n_seq, dec, _ = x_sample.shape
    depth = w_in.shape[0]
    width_a = w_up_a.shape[1]
    width_b = w_up_b.shape[1]
    kv_width = width_a + width_b
    n_heads_a = width_a // (2 * D_A)
    n_heads_b = width_b // D_B
    page = cache_k.shape[2]
    assert seq % ATT_TILE == 0 and ATT_TILE >= FAR_DISTANCE and page >= FAR_DISTANCE
    assert rel_bias.shape == (N_BUCKETS, 2 * n_heads_a + n_heads_b)

    hp = x_prompt.reshape(batch * seq, d_model)
    hs = x_sample.reshape(n_seq * dec, d_model)
    row = lambda a: a.reshape(1, -1)

    prompt_tiles = _prompt_bias_tiles(rel_bias, ATT_TILE)
    diff_tiles = _sample_bias_tiles(rel_bias, 2 * n_heads_a, lambda g: (g % 2) * n_heads_a + g // 2, dec, page)
    moba_tiles = _sample_bias_tiles(rel_bias, n_heads_b, lambda g: 2 * n_heads_a + g, dec, MOBA_BLOCK)

    k_p, v_p, k_s, v_s = [], [], [], []
    for l in range(depth):
        lam_init = 0.8 - 0.6 * math.exp(-0.3 * l)
        lam_vecs = [row(a[l]) for a in (lambda_q1, lambda_k1, lambda_q2, lambda_k2)]
        gain = row(subln_gain[l])

        def ffn_weights(wg, wu, wd):
            return (_pad_cols(wg[l], FFN_COL_TILE).astype(BF16), _pad_cols(wu[l], FFN_COL_TILE).astype(BF16),
                    _pad_rows_to(wd[l], FFN_COL_TILE).astype(BF16))

        w1 = ffn_weights(ffn1_w_gate, ffn1_w_up, ffn1_w_down)
        hp = _ffn(hp, row(ffn1_norm[l]), *w1)
        hs = _ffn(hs, row(ffn1_norm[l]), *w1)

        w_in_b = w_in[l].astype(BF16)
        qp, kp, vp, kpb, vpb, sap, sbp = _project(hp, row(mix_norm[l]), w_in_b, kv_width)
        qs, ks, vs, _, _, sas, sbs = _project(hs, row(mix_norm[l]), w_in_b, kv_width)

        d_ff_pad = w1[0].shape[1]
        oa_p, wg2_b, wu2_b = _prompt_attention(
            "diff", rel_bias, prompt_tiles, qp, kpb, vpb, batch, seq, width_a, lam_vecs + [gain],
            [(ffn2_w_gate[l], d_ff_pad), (ffn2_w_up[l], d_ff_pad)], lam_init)
        kmean_p = _prompt_block_means(kp, width_a, width_b).reshape(batch, seq // MOBA_BLOCK, width_b)
        ob_p, wa_b, wb_b, wo_b = _prompt_attention(
            "moba", rel_bias, prompt_tiles, qp, kpb, vpb, batch, seq, width_a, [kmean_p],
            [(w_up_a[l], d_model), (w_up_b[l], d_model), (w_out[l], d_model)])

        qs3 = qs.reshape(n_seq, dec, kv_width)
        ks3 = ks.reshape(n_seq, dec, kv_width)
        vs3 = vs.reshape(n_seq, dec, kv_width)
        oa_s = _sample_diff(page_table, cache_k, cache_v, l, qs3, ks3, vs3, diff_tiles, lam_vecs, gain,
                            width_a, lam_init)
        ob_s = _sample_moba(page_table, cache_k, cache_v, l, qs3, ks3, vs3, moba_tiles, width_a, width_b)

        hp = _merge(oa_p, ob_p, sap, sbp, hp, wa_b, wb_b, wo_b)
        hs = _merge(oa_s.reshape(n_seq * dec, width_a), ob_s.reshape(n_seq * dec, width_b), sas, sbs, hs,
                    wa_b, wb_b, wo_b)

        w2 = (wg2_b, wu2_b, _pad_rows_to(ffn2_w_down[l], FFN_COL_TILE).astype(BF16))
        last = l == depth - 1
        fg = row(final_norm) if last else None
        hp = _ffn(hp, row(ffn2_norm[l]), *w2, final_g=fg)
        hs = _ffn(hs, row(ffn2_norm[l]), *w2, final_g=fg)

        k_p.append(kp.reshape(batch, seq, kv_width))
        v_p.append(vp.reshape(batch, seq, kv_width))
        k_s.append(ks3)
        v_s.append(vs3)

    return (hp.reshape(batch, seq, d_model), hs.reshape(n_seq, dec, d_model),
            jnp.stack(k_p, axis=1), jnp.stack(v_p, axis=1), jnp.stack(k_s, axis=1), jnp.stack(v_s, axis=1))
```

```python
import functools
import math

import numpy as np
import jax
import jax.numpy as jnp
from jax import lax
from jax.experimental import pallas as pl
from jax.experimental.pallas import tpu as pltpu

F32 = jnp.float32
BF16 = jnp.bfloat16
NEG_INF = float("-inf")

D_A = 64
D_B = 64
MOBA_BLOCK = 256
MOBA_TOPK = 3
N_BUCKETS = 32
MAX_DISTANCE = 128
EPS = 1e-6
QK_SCALE = D_A ** -0.5

LANES = 128
SUBLANES = 8
VMEM_LIMIT_BYTES = 56 * 1024 * 1024

ATT_TILE = MOBA_BLOCK
FFN_COL_TILE = 512
PROJ_COL_TILE = 512
SAMPLE_DIFF_PAGES = 16
SAMPLE_MOBA_PAGES = 16


def _bucket_thresholds():
    n = np.arange(0, 4 * MAX_DISTANCE, dtype=np.int32)
    max_exact = N_BUCKETS // 2
    nf = np.maximum(n, max_exact).astype(np.float32)
    log_b = max_exact + (np.log(nf / np.float32(max_exact)) / np.float32(math.log(MAX_DISTANCE / max_exact))
                         * np.float32(N_BUCKETS - max_exact)).astype(np.int32)
    bucket = np.where(n < max_exact, n, np.minimum(log_b, N_BUCKETS - 1))
    assert bucket[-1] == N_BUCKETS - 1
    return [int(np.argmax(bucket >= b)) for b in range(1, N_BUCKETS)]


BUCKET_THR = _bucket_thresholds()
FAR_DISTANCE = BUCKET_THR[-1]


def _params(*sem):
    return pltpu.CompilerParams(dimension_semantics=sem, vmem_limit_bytes=VMEM_LIMIT_BYTES)


def _rms(x):
    return x * lax.rsqrt(jnp.mean(x * x, axis=-1, keepdims=True) + EPS)


def _dot(a, b):
    return jnp.dot(a, b, preferred_element_type=F32)


def _dot_nt(a, b, precision=None):
    return lax.dot_general(a, b, (((1,), (1,)), ((), ())), preferred_element_type=F32, precision=precision)


def _row_tile(n, pref):
    t = min(n, pref)
    assert n % t == 0
    return t


def _ffn_kernel(x_ref, g_ref, wg_ref, wu_ref, wd_ref, *rest, n_col, final):
    if final:
        fg_ref, o_ref, hn_ref = rest
    else:
        o_ref, hn_ref = rest
    j = pl.program_id(1)

    @pl.when(j == 0)
    def _():
        hn_ref[...] = (_rms(x_ref[...]) * g_ref[...]).astype(BF16)
        o_ref[...] = jnp.zeros_like(o_ref)

    hn = hn_ref[...]
    a = _dot(hn, wg_ref[...])
    b = _dot(hn, wu_ref[...])
    act = (a * jax.nn.sigmoid(a) * b).astype(BF16)
    o_ref[...] += _dot(act, wd_ref[...])

    @pl.when(j == n_col - 1)
    def _():
        h = x_ref[...] + 0.5 * o_ref[...]
        if final:
            h = _rms(h) * fg_ref[...]
        o_ref[...] = h


def _ffn(x, g, wg, wu, wd, final_g=None):
    t, d = x.shape
    f = wg.shape[1]
    tm = _row_tile(t, 512)
    tf = FFN_COL_TILE
    n_col = f // tf
    final = final_g is not None
    in_specs = [
        pl.BlockSpec((tm, d), lambda i, j: (i, 0)),
        pl.BlockSpec((1, d), lambda i, j: (0, 0)),
        pl.BlockSpec((d, tf), lambda i, j: (0, j)),
        pl.BlockSpec((d, tf), lambda i, j: (0, j)),
        pl.BlockSpec((tf, d), lambda i, j: (j, 0)),
    ]
    args = [x, g, wg, wu, wd]
    if final:
        in_specs.append(pl.BlockSpec((1, d), lambda i, j: (0, 0)))
        args.append(final_g)
    return pl.pallas_call(
        functools.partial(_ffn_kernel, n_col=n_col, final=final),
        out_shape=jax.ShapeDtypeStruct((t, d), F32),
        grid=(t // tm, n_col),
        in_specs=in_specs,
        out_specs=pl.BlockSpec((tm, d), lambda i, j: (i, 0)),
        scratch_shapes=[pltpu.VMEM((tm, d), BF16)],
        compiler_params=_params("parallel", "arbitrary"),
        name="ffn_final" if final else "ffn",
    )(*args)


def _proj_kernel(h_ref, g_ref, wq_ref, wk_ref, wv_ref, wa_ref, wb_ref,
                 q_ref, k_ref, v_ref, kb_ref, vb_ref, sa_ref, sb_ref, hn_ref):
    @pl.when(pl.program_id(1) == 0)
    def _():
        hn_ref[...] = (_rms(h_ref[...]) * g_ref[...]).astype(BF16)

    hn = hn_ref[...]
    q_ref[...] = (_dot(hn, wq_ref[...]) * QK_SCALE).astype(BF16)
    k = _dot(hn, wk_ref[...])
    k_ref[...] = k
    kb_ref[...] = k.astype(BF16)
    v = _dot(hn, wv_ref[...])
    v_ref[...] = v
    vb_ref[...] = v.astype(BF16)
    sa_ref[...] = jax.nn.sigmoid(_dot(hn, wa_ref[...])).astype(BF16)
    sb_ref[...] = jax.nn.sigmoid(_dot(hn, wb_ref[...])).astype(BF16)


def _project(h, g, w_in, kv_width):
    t, d = h.shape
    assert kv_width == d and w_in.shape[1] == 5 * d
    tm = _row_tile(t, 512)
    tn = _row_tile(d, PROJ_COL_TILE)
    n_col = d // tn
    w_specs = [pl.BlockSpec((d, tn), functools.partial(lambda i, j, c: (0, c * n_col + j), c=c))
               for c in range(5)]
    out_spec = pl.BlockSpec((tm, tn), lambda i, j: (i, j))
    out_shape = [jax.ShapeDtypeStruct((t, d), dt) for dt in (BF16, F32, F32, BF16, BF16, BF16, BF16)]
    return pl.pallas_call(
        _proj_kernel,
        out_shape=out_shape,
        grid=(t // tm, n_col),
        in_specs=[pl.BlockSpec((tm, d), lambda i, j: (i, 0)),
                  pl.BlockSpec((1, d), lambda i, j: (0, 0))] + w_specs,
        out_specs=[out_spec] * 7,
        scratch_shapes=[pltpu.VMEM((tm, d), BF16)],
        compiler_params=_params("parallel", "arbitrary"),
        name="project",
    )(h, g, w_in, w_in, w_in, w_in, w_in)


def _bias_of_distance(d, tab_ref, col):
    v = jnp.full(d.shape, tab_ref[0, col], F32)
    for b in range(1, N_BUCKETS):
        if b < N_BUCKETS - 1 and BUCKET_THR[b - 1] == BUCKET_THR[b]:
            continue
        v = jnp.where(d >= BUCKET_THR[b - 1], tab_ref[b, col], v)
    return v


def _prompt_bias_kernel(tab_ref, o_ref, *, tile):
    col = pl.program_id(0)
    i = lax.broadcasted_iota(jnp.int32, (tile, tile), 0)
    j = lax.broadcasted_iota(jnp.int32, (tile, tile), 1)
    o_ref[0] = _bias_of_distance(i - j, tab_ref, col)
    o_ref[1] = _bias_of_distance(i - j + tile, tab_ref, col)


def _prompt_bias_tiles(rel_bias, tile):
    n_col = rel_bias.shape[1]
    return pl.pallas_call(
        functools.partial(_prompt_bias_kernel, tile=tile),
        out_shape=jax.ShapeDtypeStruct((n_col, 2, tile, tile), F32),
        grid=(n_col,),
        in_specs=[pl.BlockSpec(memory_space=pltpu.SMEM)],
        out_specs=pl.BlockSpec((None, 2, tile, tile), lambda c: (c, 0, 0, 0)),
        compiler_params=_params("parallel"),
        name="prompt_bias_tiles",
    )(rel_bias)


def _sample_bias_kernel(tab_ref, o_ref, *, n_groups, col_of_group, dec, last_width):
    width = o_ref.shape[-1]
    i = lax.broadcasted_iota(jnp.int32, (dec, width), 0)
    j = lax.broadcasted_iota(jnp.int32, (dec, width), 1)
    for g in range(n_groups):
        col = col_of_group(g)
        rows = slice(g * dec, (g + 1) * dec)
        o_ref[0, rows, :] = _bias_of_distance(last_width + i - j, tab_ref, col)
        o_ref[1, rows, :] = jnp.full((dec, width), tab_ref[N_BUCKETS - 1, col], F32)
        new = _bias_of_distance(i - j, tab_ref, col)
        o_ref[2, rows, :] = jnp.where((j <= i) & (j < dec), new, NEG_INF)


def _sample_bias_tiles(rel_bias, n_groups, col_of_group, dec, last_width):
    width = max(last_width, LANES)
    return pl.pallas_call(
        functools.partial(_sample_bias_kernel, n_groups=n_groups, col_of_group=col_of_group, dec=dec,
                          last_width=last_width),
        out_shape=jax.ShapeDtypeStruct((3, n_groups * dec, width), F32),
        in_specs=[pl.BlockSpec(memory_space=pltpu.SMEM)],
        name="sample_bias_tiles",
    )(rel_bias)


def _lambda_value(lq1_ref, lk1_ref, lq2_ref, lk2_ref, lam_init):
    e1 = jnp.exp(jnp.sum(lq1_ref[...] * lk1_ref[...], axis=-1, keepdims=True))
    e2 = jnp.exp(jnp.sum(lq2_ref[...] * lk2_ref[...], axis=-1, keepdims=True))
    return e1 - e2 + lam_init


def _topk_block_mask(gate, k):
    n = gate.shape[-1]
    lane = lax.broadcasted_iota(jnp.int32, gate.shape, 1).astype(F32)
    mask = jnp.full(gate.shape, NEG_INF, F32)
    for _ in range(k):
        mx = jnp.max(gate, axis=-1, keepdims=True)
        first = jnp.min(jnp.where(gate == mx, lane, float(n)), axis=-1, keepdims=True)
        hit = lane == first
        mask = jnp.where(hit & (mx > NEG_INF), 0.0, mask)
        gate = jnp.where(hit, NEG_INF, gate)
    return mask


def _bf16_split_rows(mats, n_pad):
    parts = []
    for x in mats:
        rest = x
        for _ in range(3):
            piece = rest.astype(BF16).astype(F32)
            parts.append(_pad_rows(piece, n_pad))
            rest = rest - piece
    return jnp.concatenate(parts, axis=0).astype(BF16)


def _topk_rank_mask(gate, n, k):
    row_id = lax.broadcasted_iota(jnp.int32, gate.shape, 0)
    rank = jnp.zeros(gate.shape, F32)
    for other in range(n):
        cand = gate[other:other + 1, :]
        wins_tie = jnp.where(row_id > other, 1.0, 0.0)
        rank = rank + jnp.where(cand > gate, 1.0, jnp.where(cand == gate, wins_tie, 0.0))
    return jnp.where(rank < float(k), jnp.where(gate > NEG_INF, 0.0, NEG_INF), NEG_INF)


def _prompt_attn_kernel(*refs, mode, tile, n_q, n_heads_a, lam_init, n_side):
    n_main = 11 if mode == "diff" else 7
    side_in = refs[n_main:n_main + n_side]
    side_out = refs[n_main + n_side + 1:n_main + 2 * n_side + 1]
    o_ref = refs[n_main + n_side]
    s_ref, p_ref = refs[n_main + 2 * n_side + 1:]
    if mode == "diff":
        (tab_ref, q_ref, k_ref, v_ref, b1_ref, b2_ref, lq1_ref, lk1_ref, lq2_ref, lk2_ref, gain_ref) = refs[:n_main]
    else:
        (tab_ref, q_ref, k_ref, v_ref, b1_ref, b2_ref, kmean_ref) = refs[:n_main]
    for w_ref, wb_ref in zip(side_in, side_out):
        width = w_ref.shape[1]
        wb_ref[:, :width] = w_ref[...].astype(BF16)
        if wb_ref.shape[1] > width:
            wb_ref[:, width:] = jnp.zeros((wb_ref.shape[0], wb_ref.shape[1] - width), BF16)
    g = pl.program_id(1)
    half = LANES // 2
    lane = lax.broadcasted_iota(jnp.int32, (1, LANES), 1)
    bias_refs = (b1_ref, b2_ref)
    if mode == "diff":
        cols = (g, n_heads_a + g)
        lam = _lambda_value(lq1_ref, lk1_ref, lq2_ref, lk2_ref, lam_init)
    else:
        cols = (2 * n_heads_a + 2 * g, 2 * n_heads_a + 2 * g + 1)
        n_blk = kmean_ref.shape[0]
        n_pad = -(-n_blk // SUBLANES) * SUBLANES
        assert 2 * n_pad <= LANES
        kmean = kmean_ref[...]
        kmean_maps = (jnp.where(lane < half, kmean, 0.0), jnp.where(lane >= half, kmean, 0.0))
        gates = _dot_nt(_bf16_split_rows(kmean_maps, n_pad), q_ref[...])
        blk_id = lax.broadcasted_iota(jnp.int32, (n_pad, n_q * tile), 0)
        q_blk = lax.broadcasted_iota(jnp.int32, (n_pad, n_q * tile), 1) // tile
        sel_t = []
        for m in range(2):
            gate = gates[3 * m * n_pad:(3 * m + 1) * n_pad] + gates[(3 * m + 1) * n_pad:(3 * m + 2) * n_pad] \
                + gates[(3 * m + 2) * n_pad:(3 * m + 3) * n_pad]
            gate = jnp.where(blk_id < q_blk, gate, NEG_INF)
            sel_t.append(_topk_rank_mask(gate, n_blk, min(MOBA_TOPK, n_blk)))
    far = [tab_ref[N_BUCKETS - 1, cols[m]] for m in range(2)]
    row = lax.broadcasted_iota(jnp.int32, (tile, tile), 0)
    colv = lax.broadcasted_iota(jnp.int32, (tile, tile), 1)
    causal = row >= colv

    for qi in range(n_q):
        q = q_ref[qi * tile:(qi + 1) * tile, :]
        q_maps = (jnp.where(lane < half, q, jnp.zeros_like(q)), jnp.where(lane >= half, q, jnp.zeros_like(q)))
        if mode == "moba" and qi > 0:
            packed = [sel_t[m][:, qi * tile:(qi + 1) * tile] for m in range(2)]
            packed.append(jnp.zeros((LANES - 2 * n_pad, tile), F32))
            sel_q = jnp.concatenate(packed, axis=0).T
        outs = []
        q_both = jnp.concatenate(q_maps, axis=0)
        run_maxes = [None, None]
        for j in range(qi + 1):
            s_both = _dot_nt(q_both, k_ref[j * tile:(j + 1) * tile, :])
            for m in range(2):
                s = s_both[m * tile:(m + 1) * tile]
                if j == qi:
                    s = jnp.where(causal, s + bias_refs[m][0], NEG_INF)
                else:
                    shift = bias_refs[m][1] if j == qi - 1 else far[m]
                    if mode == "moba":
                        shift = shift + sel_q[:, m * n_pad + j:m * n_pad + j + 1]
                    s = s + shift
                s_ref[m, j] = s
                run_maxes[m] = s if run_maxes[m] is None else jnp.maximum(run_maxes[m], s)
        for m in range(2):
            mx = jnp.max(run_maxes[m], axis=-1, keepdims=True)
            run_sum = None
            for j in range(qi + 1):
                p = jnp.exp(s_ref[m, j] - mx)
                run_sum = p if run_sum is None else run_sum + p
                p_ref[m, :, j * tile:(j + 1) * tile] = p.astype(BF16)
            n_keys = (qi + 1) * tile
            denom = jnp.sum(run_sum, axis=-1, keepdims=True)
            outs.append(_dot(p_ref[m, :, :n_keys], v_ref[:n_keys, :]) / denom)
        rows = slice(qi * tile, (qi + 1) * tile)
        if mode == "diff":
            o = outs[0] - lam * outs[1]
            o_ref[rows, :] = (_rms(o) * gain_ref[...] * (1.0 - lam_init)).astype(o_ref.dtype)
        else:
            o_ref[rows, :] = jnp.where(lane < half, outs[0], outs[1]).astype(o_ref.dtype)


def _prompt_attention(mode, rel_bias, bias_tiles, q, kb, vb, batch, seq, width_a, extras, side_casts,
                      lam_init=0.0):
    tile = ATT_TILE
    n_q = seq // tile
    n_heads_a = width_a // (2 * D_A)
    if mode == "diff":
        n_groups = width_a // LANES
        col0 = 0
        cols = (lambda g: g, lambda g: n_heads_a + g)
    else:
        n_groups = (q.shape[1] - width_a) // LANES
        col0 = width_a // LANES
        cols = (lambda g: 2 * n_heads_a + 2 * g, lambda g: 2 * n_heads_a + 2 * g + 1)
    seq_spec = pl.BlockSpec((seq, LANES), lambda b, g: (b, col0 + g))
    in_specs = [
        pl.BlockSpec(memory_space=pltpu.SMEM),
        seq_spec, seq_spec, seq_spec,
        pl.BlockSpec((None, 2, tile, tile), lambda b, g: (cols[0](g), 0, 0, 0)),
        pl.BlockSpec((None, 2, tile, tile), lambda b, g: (cols[1](g), 0, 0, 0)),
    ]
    args = [rel_bias, q, kb, vb, bias_tiles, bias_tiles]
    if mode == "diff":
        lq1, lk1, lq2, lk2, gain = extras
        in_specs += [pl.BlockSpec((1, D_A), lambda b, g: (0, 0))] * 4
        in_specs += [pl.BlockSpec((1, LANES), lambda b, g: (0, 0))]
        args += [lq1, lk1, lq2, lk2, gain]
    else:
        (kmean,) = extras
        n_blk = kmean.shape[1]
        in_specs += [pl.BlockSpec((None, n_blk, LANES), lambda b, g: (b, 0, g))]
        args += [kmean]
    n_steps = batch * n_groups
    out_shape = [jax.ShapeDtypeStruct((batch * seq, n_groups * LANES), BF16)]
    out_specs = [pl.BlockSpec((seq, LANES), lambda b, g: (b, g))]
    for w, width_out in side_casts:
        rows = w.shape[0] // n_steps
        assert rows * n_steps == w.shape[0] and rows % (2 * SUBLANES) == 0 and width_out % LANES == 0
        in_specs.append(pl.BlockSpec((rows, w.shape[1]), lambda b, g: (b * n_groups + g, 0)))
        args.append(w)
        out_shape.append(jax.ShapeDtypeStruct((w.shape[0], width_out), BF16))
        out_specs.append(pl.BlockSpec((rows, width_out), lambda b, g: (b * n_groups + g, 0)))
    return pl.pallas_call(
        functools.partial(_prompt_attn_kernel, mode=mode, tile=tile, n_q=n_q, n_heads_a=n_heads_a,
                          lam_init=lam_init, n_side=len(side_casts)),
        out_shape=out_shape,
        grid=(batch, n_groups),
        in_specs=in_specs,
        out_specs=out_specs,
        scratch_shapes=[pltpu.VMEM((2, n_q, tile, tile), F32), pltpu.VMEM((2, tile, seq), BF16)],
        compiler_params=_params("parallel", "parallel"),
        name="prompt_" + mode,
    )(*args)


def _block_mean_kernel(k_ref, o_ref):
    o_ref[...] = jnp.mean(k_ref[...], axis=0, keepdims=True)


def _prompt_block_means(k, width_a, width_b):
    assert width_a % width_b == 0
    n_blk = k.shape[0] // MOBA_BLOCK
    return pl.pallas_call(
        _block_mean_kernel,
        out_shape=jax.ShapeDtypeStruct((n_blk, 1, width_b), F32),
        grid=(n_blk,),
        in_specs=[pl.BlockSpec((MOBA_BLOCK, width_b), lambda i: (i, width_a // width_b))],
        out_specs=pl.BlockSpec((None, 1, width_b), lambda i: (i, 0, 0)),
        compiler_params=_params("parallel"),
        name="prompt_block_means",
    )(k)


def _block_diag_queries(q, n_groups, group_width):
    lane = lax.broadcasted_iota(jnp.int32, q.shape, 1)
    rows = [jnp.where((lane >= g * group_width) & (lane < (g + 1) * group_width), q, jnp.zeros_like(q))
            for g in range(n_groups)]
    return jnp.concatenate(rows, axis=0)


def _gather_block_diag(x, n_groups, rows_per_group, group_width):
    lane = lax.broadcasted_iota(jnp.int32, (rows_per_group, x.shape[1]), 1)
    out = jnp.zeros((rows_per_group, x.shape[1]), x.dtype)
    for g in range(n_groups):
        part = x[g * rows_per_group:(g + 1) * rows_per_group, :]
        out = jnp.where((lane >= g * group_width) & (lane < (g + 1) * group_width), part, out)
    return out


def _pad_rows(x, n_rows):
    if x.shape[0] == n_rows:
        return x
    return jnp.concatenate([x, jnp.zeros((n_rows - x.shape[0], x.shape[1]), x.dtype)], axis=0)


def _softmax_start(s, v, m_ref, l_ref, acc_ref):
    mx = jnp.max(s, axis=-1, keepdims=True)
    p = jnp.exp(s - mx)
    m_ref[...] = mx
    l_ref[...] = jnp.sum(p, axis=-1, keepdims=True)
    acc_ref[...] = _dot(p.astype(BF16), v)


def _softmax_step(s, v, m_ref, l_ref, acc_ref):
    m_old = m_ref[...]
    mx = jnp.maximum(m_old, jnp.max(s, axis=-1, keepdims=True))
    alpha = jnp.exp(m_old - mx)
    p = jnp.exp(s - mx)
    l_ref[...] = alpha * l_ref[...] + jnp.sum(p, axis=-1, keepdims=True)
    acc_ref[...] = alpha * acc_ref[...] + _dot(p.astype(BF16), v)
    m_ref[...] = mx


def _sample_diff_kernel(pt_ref, *refs, n_pages_step, page, dec, n_heads, lam_init):
    k_refs = refs[:n_pages_step]
    v_refs = refs[n_pages_step:2 * n_pages_step]
    (q_ref, kn_ref, vn_ref, bias_ref, lq1_ref, lk1_ref, lq2_ref, lk2_ref, gain_ref,
     o_ref, qb_ref, m_ref, l_ref, acc_ref) = refs[2 * n_pages_step:]
    c = pl.program_id(1)
    n_steps = pl.num_programs(1)

    @pl.when(c == 0)
    def _():
        qb_ref[...] = _block_diag_queries(q_ref[...].astype(F32), 2 * n_heads, D_A).astype(BF16)
        k_new = _pad_rows(kn_ref[...], LANES).astype(BF16)
        v_new = _pad_rows(vn_ref[...], LANES).astype(BF16)
        s = _dot_nt(qb_ref[...], k_new) + bias_ref[2]
        _softmax_start(s, v_new, m_ref, l_ref, acc_ref)

    qb = qb_ref[...]
    scores = []
    for p in range(n_pages_step):
        s = _dot_nt(qb, k_refs[p][...].astype(BF16))
        if p == n_pages_step - 1:
            s = s + jnp.where(c == n_steps - 1, bias_ref[0], bias_ref[1])
        else:
            s = s + bias_ref[1]
        scores.append(s)
    s = jnp.concatenate(scores, axis=1)
    v = jnp.concatenate([v_refs[p][...].astype(BF16) for p in range(n_pages_step)], axis=0)
    _softmax_step(s, v, m_ref, l_ref, acc_ref)

    @pl.when(c == n_steps - 1)
    def _():
        on = acc_ref[...] / l_ref[...]
        lane = lax.broadcasted_iota(jnp.int32, (dec, on.shape[1]), 1)
        o1 = jnp.zeros((dec, on.shape[1]), F32)
        o2 = jnp.zeros((dec, on.shape[1]), F32)
        for h in range(n_heads):
            in_head = (lane >= h * 2 * D_A) & (lane < (h + 1) * 2 * D_A)
            o1 = jnp.where(in_head, on[(2 * h) * dec:(2 * h + 1) * dec, :], o1)
            o2 = jnp.where(in_head, on[(2 * h + 1) * dec:(2 * h + 2) * dec, :], o2)
        lam = _lambda_value(lq1_ref, lk1_ref, lq2_ref, lk2_ref, lam_init)
        o = o1 - lam * o2
        for h in range(n_heads):
            cols = slice(h * 2 * D_A, (h + 1) * 2 * D_A)
            o_ref[:, cols] = (_rms(o[:, cols]) * gain_ref[...] * (1.0 - lam_init)).astype(o_ref.dtype)


def _sample_diff(page_table, cache_k, cache_v, layer, q, k_new, v_new, bias, lam_vecs, gain, width_a, lam_init):
    n_seq, n_pages = page_table.shape
    page = cache_k.shape[2]
    dec = q.shape[1]
    n_heads = width_a // (2 * D_A)
    pps = SAMPLE_DIFF_PAGES if n_pages % SAMPLE_DIFF_PAGES == 0 else 1
    n_steps = n_pages // pps
    n_rows = 2 * n_heads * dec
    assert page == LANES and 2 * D_A == LANES and bias.shape == (3, n_rows, LANES)

    def page_spec(p):
        return pl.BlockSpec((None, None, page, width_a),
                            lambda b, c, pt: (pt[b * n_pages + c * pps + p], layer, 0, 0))

    row_spec = pl.BlockSpec((None, dec, width_a), lambda b, c, pt: (b, 0, 0))
    vec_spec = pl.BlockSpec((1, D_A), lambda b, c, pt: (0, 0))
    in_specs = ([page_spec(p) for p in range(pps)] * 2
                + [row_spec, row_spec, row_spec,
                   pl.BlockSpec((3, n_rows, LANES), lambda b, c, pt: (0, 0, 0)),
                   vec_spec, vec_spec, vec_spec, vec_spec,
                   pl.BlockSpec((1, LANES), lambda b, c, pt: (0, 0))])
    return pl.pallas_call(
        functools.partial(_sample_diff_kernel, n_pages_step=pps, page=page, dec=dec, n_heads=n_heads,
                          lam_init=lam_init),
        out_shape=jax.ShapeDtypeStruct((n_seq, dec, width_a), BF16),
        grid_spec=pltpu.PrefetchScalarGridSpec(
            num_scalar_prefetch=1,
            grid=(n_seq, n_steps),
            in_specs=in_specs,
            out_specs=pl.BlockSpec((None, dec, width_a), lambda b, c, pt: (b, 0, 0)),
            scratch_shapes=[pltpu.VMEM((n_rows, width_a), BF16), pltpu.VMEM((n_rows, 1), F32),
                            pltpu.VMEM((n_rows, 1), F32), pltpu.VMEM((n_rows, width_a), F32)]),
        compiler_params=_params("parallel", "arbitrary"),
        name="sample_diff",
    )(page_table.reshape(-1), *([cache_k] * pps), *([cache_v] * pps), q, k_new, v_new, bias, *lam_vecs, gain)


def _sample_moba_kernel(pt_ref, *refs, n_seq, n_pages_step, n_steps, dec, n_heads, n_blk, pages_per_block):
    k_refs = refs[:n_pages_step]
    v_refs = refs[n_pages_step:2 * n_pages_step]
    (q_ref, kn_ref, vn_ref, bias_ref,
     o_ref, qb_ref, s_ref, smax_ref, kmean_ref, sel_ref, m_ref, l_ref, acc_ref) = refs[2 * n_pages_step:]
    b = pl.program_id(0)
    c = pl.program_id(1)
    in_key_phase = b < n_seq
    in_value_phase = b >= 1
    page = k_refs[0].shape[0]
    blocks_step = n_pages_step // pages_per_block
    blk_lane = lax.broadcasted_iota(jnp.int32, smax_ref.shape, 1)

    @pl.when(jnp.logical_and(in_value_phase, c == 0))
    def _():
        qb = qb_ref[...]
        parts = _dot_nt(qb, _bf16_split_rows([kmean_ref[...]], LANES))
        gate = parts[:, :LANES] + parts[:, LANES:2 * LANES] + parts[:, 2 * LANES:]
        gate = jnp.where(blk_lane < n_blk, gate, NEG_INF)
        sel = _topk_block_mask(gate, min(MOBA_TOPK, n_blk))
        sel_ref[...] = sel
        k_new = _pad_rows(kn_ref[...], LANES).astype(BF16)
        v_new = _pad_rows(vn_ref[...], LANES).astype(BF16)
        s_new = _dot_nt(qb, k_new) + bias_ref[2][:, :LANES]
        mx = jnp.maximum(jnp.max(s_new, axis=-1, keepdims=True),
                         jnp.max(smax_ref[...] + sel, axis=-1, keepdims=True))
        p = jnp.exp(s_new - mx)
        m_ref[...] = mx
        l_ref[...] = jnp.sum(p, axis=-1, keepdims=True)
        acc_ref[...] = _dot(p.astype(BF16), v_new)

    @pl.when(in_value_phase)
    def _():
        sel = sel_ref[...]
        mx = m_ref[...]
        parts = []
        run_sum = None
        for blk in range(blocks_step):
            picked = jnp.max(jnp.where(blk_lane == c * blocks_step + blk, sel, NEG_INF), axis=-1, keepdims=True)
            p = jnp.exp(s_ref[c, :, blk * MOBA_BLOCK:(blk + 1) * MOBA_BLOCK] + (picked - mx))
            run_sum = p if run_sum is None else run_sum + p
            parts.append(p.astype(BF16))
        p = jnp.concatenate(parts, axis=1)
        v = jnp.concatenate([v_refs[i][...].astype(BF16) for i in range(n_pages_step)], axis=0)
        l_ref[...] += jnp.sum(run_sum, axis=-1, keepdims=True)
        acc_ref[...] += _dot(p, v)

    @pl.when(jnp.logical_and(in_value_phase, c == n_steps - 1))
    def _():
        on = acc_ref[...] / l_ref[...]
        o_ref[...] = _gather_block_diag(on, n_heads, dec, D_B).astype(o_ref.dtype)

    @pl.when(jnp.logical_and(in_key_phase, c == 0))
    def _():
        qb_ref[...] = _block_diag_queries(q_ref[...].astype(F32), n_heads, D_B).astype(BF16)
        smax_ref[...] = jnp.full(smax_ref.shape, NEG_INF, F32)
        kmean_ref[...] = jnp.zeros_like(kmean_ref)

    @pl.when(in_key_phase)
    def _():
        qb = qb_ref[...]
        far = bias_ref[1][:, :page]
        smax = smax_ref[...]
        for blk in range(blocks_step):
            blk_max = None
            for p in range(blk * pages_per_block, (blk + 1) * pages_per_block):
                s = _dot_nt(qb, k_refs[p][...].astype(BF16))
                in_last_block = p - (n_pages_step - pages_per_block)
                if in_last_block >= 0:
                    near = bias_ref[0][:, in_last_block * page:(in_last_block + 1) * page]
                    s = s + jnp.where(c == n_steps - 1, near, far)
                else:
                    s = s + far
                s_ref[c, :, p * page:(p + 1) * page] = s
                blk_max = s if blk_max is None else jnp.maximum(blk_max, s)
            smax = jnp.where(blk_lane == c * blocks_step + blk, jnp.max(blk_max, axis=-1, keepdims=True), smax)
        smax_ref[...] = smax
        means = []
        for blk in range(blocks_step):
            total = k_refs[blk * pages_per_block][...]
            for p in range(1, pages_per_block):
                total = total + k_refs[blk * pages_per_block + p][...]
            means.append(jnp.sum(total, axis=0, keepdims=True) * (1.0 / MOBA_BLOCK))
        first_blk = pl.multiple_of(c * blocks_step, blocks_step)
        kmean_ref[pl.ds(first_blk, blocks_step), :] = jnp.concatenate(means, axis=0)


def _sample_moba(page_table, cache_k, cache_v, layer, q, k_new, v_new, bias, width_a, width_b):
    n_seq, n_pages = page_table.shape
    page = cache_k.shape[2]
    dec = q.shape[1]
    n_heads = width_b // D_B
    pps = SAMPLE_MOBA_PAGES
    ppb = MOBA_BLOCK // page
    n_blk = n_pages // ppb
    assert n_pages % pps == 0 and pps // ppb == SUBLANES and width_a % width_b == 0 and n_blk <= LANES
    n_steps = n_pages // pps
    n_rows = n_heads * dec
    col_blk = width_a // width_b
    assert bias.shape == (3, n_rows, MOBA_BLOCK)

    def key_spec(p):
        return pl.BlockSpec((None, None, page, width_b), lambda b, c, pt: (
            pt[jnp.minimum(b, n_seq - 1) * n_pages + jnp.where(b < n_seq, c, n_steps - 1) * pps + p],
            layer, 0, col_blk))

    def value_spec(p):
        return pl.BlockSpec((None, None, page, width_b), lambda b, c, pt: (
            pt[jnp.maximum(b - 1, 0) * n_pages + jnp.where(b >= 1, c, 0) * pps + p], layer, 0, col_blk))

    new_seq_spec = pl.BlockSpec((None, dec, width_b), lambda b, c, pt: (jnp.minimum(b, n_seq - 1), 0, col_blk))
    old_seq_spec = pl.BlockSpec((None, dec, width_b), lambda b, c, pt: (jnp.maximum(b - 1, 0), 0, col_blk))
    return pl.pallas_call(
        functools.partial(_sample_moba_kernel, n_seq=n_seq, n_pages_step=pps, n_steps=n_steps, dec=dec,
                          n_heads=n_heads, n_blk=n_blk, pages_per_block=ppb),
        out_shape=jax.ShapeDtypeStruct((n_seq, dec, width_b), BF16),
        grid_spec=pltpu.PrefetchScalarGridSpec(
            num_scalar_prefetch=1,
            grid=(n_seq + 1, n_steps),
            in_specs=[key_spec(p) for p in range(pps)] + [value_spec(p) for p in range(pps)]
            + [new_seq_spec, old_seq_spec, old_seq_spec,
               pl.BlockSpec((3, n_rows, MOBA_BLOCK), lambda b, c, pt: (0, 0, 0))],
            out_specs=pl.BlockSpec((None, dec, width_b), lambda b, c, pt: (jnp.maximum(b - 1, 0), 0, 0)),
            scratch_shapes=[pltpu.VMEM((n_rows, width_b), BF16),
                            pltpu.VMEM((n_steps, n_rows, pps * page), F32),
                            pltpu.VMEM((n_rows, LANES), F32),
                            pltpu.VMEM((LANES, width_b), F32),
                            pltpu.VMEM((n_rows, LANES), F32),
                            pltpu.VMEM((n_rows, 1), F32), pltpu.VMEM((n_rows, 1), F32),
                            pltpu.VMEM((n_rows, width_b), F32)]),
        compiler_params=_params("arbitrary", "arbitrary"),
        name="sample_moba",
    )(page_table.reshape(-1), *([cache_k] * pps), *([cache_v] * pps), q, k_new, v_new, bias)


def _merge_kernel(oa_ref, ob_ref, sa_ref, sb_ref, h_ref, wa_ref, wb_ref, wo_ref, o_ref):
    m = (sa_ref[...].astype(F32) * _dot(oa_ref[...], wa_ref[...])
         + sb_ref[...].astype(F32) * _dot(ob_ref[...], wb_ref[...]))
    o_ref[...] = h_ref[...] + _dot(m.astype(BF16), wo_ref[...])


def _merge(oa, ob, sa, sb, h, w_up_a, w_up_b, w_out):
    t, d = h.shape
    tm = _row_tile(t, 256)
    wa, wb = oa.shape[1], ob.shape[1]
    row = lambda w: pl.BlockSpec((tm, w), lambda i: (i, 0))
    full = lambda a: pl.BlockSpec(a.shape, lambda i: (0, 0))
    return pl.pallas_call(
        _merge_kernel,
        out_shape=jax.ShapeDtypeStruct((t, d), F32),
        grid=(t // tm,),
        in_specs=[row(wa), row(wb), row(d), row(d), row(d), full(w_up_a), full(w_up_b), full(w_out)],
        out_specs=row(d),
        compiler_params=_params("parallel"),
        name="merge",
    )(oa, ob, sa, sb, h, w_up_a, w_up_b, w_out)


def _pad_cols(w, mult):
    pad = (-w.shape[1]) % mult
    return jnp.pad(w, ((0, 0), (0, pad)))


def _pad_rows_to(w, mult):
    pad = (-w.shape[0]) % mult
    return jnp.pad(w, ((0, pad), (0, 0)))


def kernel(x_prompt, x_sample, cache_k, cache_v, page_table, rel_bias, ffn1_norm, ffn1_w_gate, ffn1_w_up,
           ffn1_w_down, mix_norm, w_in, lambda_q1, lambda_k1, lambda_q2, lambda_k2, subln_gain, w_up_a,
           w_up_b, w_out, ffn2_norm, ffn2_w_gate, ffn2_w_up, ffn2_w_down, final_norm):
    batch, seq, d_model = x_prompt.shape
    n_seq, dec, _ = x_sample.shape
    depth = w_in.shape[0]
    width_a = w_up_a.shape[1]
    width_b = w_up_b.shape[1]
    kv_width = width_a + width_b
    n_heads_a = width_a // (2 * D_A)
    n_heads_b = width_b // D_B
    page = cache_k.shape[2]
    assert seq % ATT_TILE == 0 and ATT_TILE >= FAR_DISTANCE and page >= FAR_DISTANCE
    assert rel_bias.shape == (N_BUCKETS, 2 * n_heads_a + n_heads_b)

    hp = x_prompt.reshape(batch * seq, d_model)
    hs = x_sample.reshape(n_seq * dec, d_model)
    row = lambda a: a.reshape(1, -1)

    prompt_tiles = _prompt_bias_tiles(rel_bias, ATT_TILE)
    diff_tiles = _sample_bias_tiles(rel_bias, 2 * n_heads_a, lambda g: (g % 2) * n_heads_a + g // 2, dec, page)
    moba_tiles = _sample_bias_tiles(rel_bias, n_heads_b, lambda g: 2 * n_heads_a + g, dec, MOBA_BLOCK)

    k_p, v_p, k_s, v_s = [], [], [], []
    for l in range(depth):
        lam_init = 0.8 - 0.6 * math.exp(-0.3 * l)
        lam_vecs = [row(a[l]) for a in (lambda_q1, lambda_k1, lambda_q2, lambda_k2)]
        gain = row(subln_gain[l])

        def ffn_weights(wg, wu, wd):
            return (_pad_cols(wg[l], FFN_COL_TILE).astype(BF16), _pad_cols(wu[l], FFN_COL_TILE).astype(BF16),
                    _pad_rows_to(wd[l], FFN_COL_TILE).astype(BF16))

        w1 = ffn_weights(ffn1_w_gate, ffn1_w_up, ffn1_w_down)
        hp = _ffn(hp, row(ffn1_norm[l]), *w1)
        hs = _ffn(hs, row(ffn1_norm[l]), *w1)

        w_in_b = w_in[l].astype(BF16)
        qp, kp, vp, kpb, vpb, sap, sbp = _project(hp, row(mix_norm[l]), w_in_b, kv_width)
        qs, ks, vs, _, _, sas, sbs = _project(hs, row(mix_norm[l]), w_in_b, kv_width)

        d_ff_pad = w1[0].shape[1]
        oa_p, wg2_b, wu2_b = _prompt_attention(
            "diff", rel_bias, prompt_tiles, qp, kpb, vpb, batch, seq, width_a, lam_vecs + [gain],
            [(ffn2_w_gate[l], d_ff_pad), (ffn2_w_up[l], d_ff_pad)], lam_init)
        kmean_p = _prompt_block_means(kp, width_a, width_b).reshape(batch, seq // MOBA_BLOCK, width_b)
        ob_p, wa_b, wb_b, wo_b = _prompt_attention(
            "moba", rel_bias, prompt_tiles, qp, kpb, vpb, batch, seq, width_a, [kmean_p],
            [(w_up_a[l], d_model), (w_up_b[l], d_model), (w_out[l], d_model)])

        qs3 = qs.reshape(n_seq, dec, kv_width)
        ks3 = ks.reshape(n_seq, dec, kv_width)
        vs3 = vs.reshape(n_seq, dec, kv_width)
        oa_s = _sample_diff(page_table, cache_k, cache_v, l, qs3, ks3, vs3, diff_tiles, lam_vecs, gain,
                            width_a, lam_init)
        ob_s = _sample_moba(page_table, cache_k, cache_v, l, qs3, ks3, vs3, moba_tiles, width_a, width_b)

        hp = _merge(oa_p, ob_p, sap, sbp, hp, wa_b, wb_b, wo_b)
        hs = _merge(oa_s.reshape(n_seq * dec, width_a), ob_s.reshape(n_seq * dec, width_b), sas, sbs, hs,
                    wa_b, wb_b, wo_b)

        w2 = (wg2_b, wu2_b, _pad_rows_to(ffn2_w_down[l], FFN_COL_TILE).astype(BF16))
        last = l == depth - 1
        fg = row(final_norm) if last else None
        hp = _ffn(hp, row(ffn2_norm[l]), *w2, final_g=fg)
        hs = _ffn(hs, row(ffn2_norm[l]), *w2, final_g=fg)

        k_p.append(kp.reshape(batch, seq, kv_width))
        v_p.append(vp.reshape(batch, seq, kv_width))
        k_s.append(ks3)
        v_s.append(vs3)

    return (hp.reshape(batch, seq, d_model), hs.reshape(n_seq, dec, d_model),
            jnp.stack(k_p, axis=1), jnp.stack(v_p, axis=1), jnp.stack(k_s, axis=1), jnp.stack(v_s, axis=1))
```
